```python
import jax, jax.numpy as jnp
from jax import lax
import numpy as np

D_MODEL = 1024
BATCH = 4
SEQ = 8192
DEPTH = 2

N_MIXERS = 2
HEAD_DIM = 64
N_Q_HEADS = D_MODEL // HEAD_DIM
N_KV_HEADS = N_Q_HEADS // 4
GROUP = N_Q_HEADS // N_KV_HEADS
Q_DIM = N_Q_HEADS * HEAD_DIM
KV_DIM = N_KV_HEADS * HEAD_DIM
QKV_DIM = Q_DIM + 2 * KV_DIM
WINDOW = 128
BLOCK = 128
ROT_DIM = HEAD_DIM // 4
ROPE_THETA = 500000.0
NEG_INF = -1e30
HGRN_DK = 128
HGRN_HEADS = D_MODEL // HGRN_DK
HGRN_DV = D_MODEL // HGRN_HEADS
CHUNK = 64
D_FF = 4 * D_MODEL
NORM_EPS = 1e-5
N_ATTN_LAYERS = (DEPTH + N_MIXERS - 1) // N_MIXERS
N_HGRN_LAYERS = DEPTH // N_MIXERS

kernel_name = "hybrid_swa_sink_hgrn2_sqrelu"


def rmsnorm(x, gain):
    xf = x.astype(jnp.float32)
    y = xf * lax.rsqrt(jnp.mean(jnp.square(xf), axis=-1, keepdims=True) + NORM_EPS)
    return (y * gain.astype(jnp.float32)).astype(x.dtype)


def rotary_tables(positions):
    inv_freq = ROPE_THETA ** (-jnp.arange(0, ROT_DIM, 2, dtype=jnp.float32) / ROT_DIM)
    ang = positions.astype(jnp.float32)[..., None] * inv_freq
    return jnp.cos(ang), jnp.sin(ang)


def apply_partial_rotary(x, cos, sin):
    half = ROT_DIM // 2
    cos = cos.astype(x.dtype)
    sin = sin.astype(x.dtype)
    x1 = x[..., :half]
    x2 = x[..., half:ROT_DIM]
    return jnp.concatenate([x1 * cos - x2 * sin, x2 * cos + x1 * sin, x[..., ROT_DIM:]], axis=-1)


def sliding_window_attention(h, positions, w_qkv, b_qkv, sinks, w_o):
    bsz, seq, _ = h.shape
    nb = seq // BLOCK
    qkv = h @ w_qkv + b_qkv
    q = qkv[..., :Q_DIM].reshape(bsz, seq, N_KV_HEADS, GROUP, HEAD_DIM)
    k = qkv[..., Q_DIM:Q_DIM + KV_DIM].reshape(bsz, seq, N_KV_HEADS, HEAD_DIM)
    v = qkv[..., Q_DIM + KV_DIM:].reshape(bsz, seq, N_KV_HEADS, HEAD_DIM)
    cos, sin = rotary_tables(positions)
    q = apply_partial_rotary(q, cos[:, :, None, None, :], sin[:, :, None, None, :])
    k = apply_partial_rotary(k, cos[:, :, None, :], sin[:, :, None, :])
    qb = q.reshape(bsz, nb, BLOCK, N_KV_HEADS, GROUP, HEAD_DIM)

    def band(t):
        tp = jnp.pad(t, ((0, 0), (BLOCK, 0), (0, 0), (0, 0)))
        tp = tp.reshape(bsz, nb + 1, BLOCK, N_KV_HEADS, HEAD_DIM)
        return jnp.concatenate([tp[:, :-1], tp[:, 1:]], axis=2)

    kb = band(k)
    vb = band(v)
    scores = jnp.einsum('bnqkgd,bnskd->bnkgqs', qb, kb).astype(jnp.float32) * (HEAD_DIM ** -0.5)
    qi = jnp.arange(BLOCK)[:, None]
    kj = jnp.arange(2 * BLOCK)[None, :]
    delta = qi + BLOCK - kj
    key_pos = jnp.arange(nb)[:, None, None] * BLOCK + kj[None] - BLOCK
    mask = (delta >= 0) & (delta < WINDOW) & (key_pos >= 0)
    scores = jnp.where(mask[None, :, None, None], scores, NEG_INF)
    sink = sinks.astype(jnp.float32).reshape(1, 1, N_KV_HEADS, GROUP, 1, 1)
    m = jnp.maximum(jnp.max(scores, axis=-1, keepdims=True), sink)
    e = jnp.exp(scores - m)
    probs = e / (jnp.sum(e, axis=-1, keepdims=True) + jnp.exp(sink - m))
    out = jnp.einsum('bnkgqs,bnskd->bnqkgd', probs.astype(vb.dtype), vb)
    return out.reshape(bsz, seq, Q_DIM) @ w_o


def hgrn2_recurrence(h, lower_bound, w_in, g_norm, w_o):
    bsz, seq, _ = h.shape
    nc = seq // CHUNK
    q, f, i, g = jnp.split(h @ w_in, 4, axis=-1)
    q = jax.nn.silu(q.astype(jnp.float32))
    lb = lower_bound.astype(jnp.float32)
    forget = lb + (1.0 - lb) * jax.nn.sigmoid(f.astype(jnp.float32))
    k = 1.0 - forget
    log_f = jnp.log(forget)

    def to_chunks(t, d):
        return t.reshape(bsz, nc, CHUNK, HGRN_HEADS, d).transpose(1, 0, 3, 2, 4)

    xs = (to_chunks(q, HGRN_DK), to_chunks(k, HGRN_DK),
          to_chunks(i.astype(jnp.float32), HGRN_DV), to_chunks(log_f, HGRN_DK))
    causal = jnp.tril(jnp.ones((CHUNK, CHUNK), dtype=bool))

    def chunk_step(state, inp):
        qc, kc, vc, lc = inp
        b = jnp.cumsum(lc, axis=2)
        o_inter = jnp.einsum('bhtd,bhde->bhte', qc * jnp.exp(b), state)
        diff = b[:, :, :, None, :] - b[:, :, None, :, :]
        decay = jnp.exp(jnp.where(causal[None, None, :, :, None], diff, -jnp.inf))
        scores = jnp.einsum('bhtd,bhtsd,bhsd->bhts', qc, decay, kc)
        o_intra = jnp.einsum('bhts,bhse->bhte', scores, vc)
        b_last = b[:, :, -1]
        new_state = jnp.exp(b_last)[..., None] * state + jnp.einsum(
            'bhsd,bhse->bhde', kc * jnp.exp(b_last[:, :, None, :] - b), vc)
        return new_state, o_inter + o_intra

    state0 = jnp.zeros((bsz, HGRN_HEADS, HGRN_DK, HGRN_DV), jnp.float32)
    _, o = lax.scan(chunk_step, state0, xs)
    o = o.transpose(1, 0, 3, 2, 4).reshape(bsz, seq, HGRN_HEADS * HGRN_DV)
    o = rmsnorm(o, g_norm) * jax.nn.silu(g.astype(jnp.float32))
    return o.astype(h.dtype) @ w_o


def sqrelu_mlp(h, w_up, w_down):
    return jnp.square(jax.nn.relu(h @ w_up)) @ w_down


def setup_inputs(seed: int = 0) -> dict:
    key = jax.random.key(seed)
    ks = jax.random.split(key, 16)
    f32 = jnp.float32

    def nrm(k, shape, scale):
        return jax.random.normal(k, shape, f32) * scale

    x = jax.random.normal(ks[0], (BATCH, SEQ, D_MODEL), f32)
    positions = jnp.broadcast_to(jnp.arange(SEQ, dtype=jnp.int32)[None, :], (BATCH, SEQ))
    return {
        "x": x,
        "positions": positions,
        "mix_norm": 1.0 + nrm(ks[1], (DEPTH, D_MODEL), 0.02),
        "mlp_norm": 1.0 + nrm(ks[2], (DEPTH, D_MODEL), 0.02),
        "final_norm": 1.0 + nrm(ks[3], (D_MODEL,), 0.02),
        "attn_w_qkv": nrm(ks[4], (N_ATTN_LAYERS, D_MODEL, QKV_DIM), D_MODEL ** -0.5),
        "attn_b_qkv": nrm(ks[5], (N_ATTN_LAYERS, QKV_DIM), 0.02),
        "attn_sinks": nrm(ks[6], (N_ATTN_LAYERS, N_Q_HEADS), 0.5),
        "attn_w_o": nrm(ks[7], (N_ATTN_LAYERS, Q_DIM, D_MODEL), Q_DIM ** -0.5),
        "hgrn_w_in": nrm(ks[8], (N_HGRN_LAYERS, D_MODEL, 4 * D_MODEL), D_MODEL ** -0.5),
        "hgrn_g_norm": 1.0 + nrm(ks[9], (N_HGRN_LAYERS, HGRN_HEADS * HGRN_DV), 0.02),
        "hgrn_w_o": nrm(ks[10], (N_HGRN_LAYERS, HGRN_HEADS * HGRN_DV, D_MODEL), D_MODEL ** -0.5),
        "hgrn_lower_bounds": nrm(ks[11], (DEPTH, HGRN_HEADS * HGRN_DK), 0.1),
        "mlp_w_up": nrm(ks[12], (DEPTH, D_MODEL, D_FF), D_MODEL ** -0.5),
        "mlp_w_down": nrm(ks[13], (DEPTH, D_FF, D_MODEL), D_FF ** -0.5),
    }


def reference(x, positions, mix_norm, mlp_norm, final_norm, attn_w_qkv, attn_b_qkv, attn_sinks,
              attn_w_o, hgrn_w_in, hgrn_g_norm, hgrn_w_o, hgrn_lower_bounds, mlp_w_up, mlp_w_down):
    lbs = jnp.cumsum(jax.nn.softmax(hgrn_lower_bounds.astype(jnp.float32), axis=0), axis=0)
    lbs = lbs - lbs[0:1]
    for layer in range(DEPTH):
        j = layer // N_MIXERS
        h = rmsnorm(x, mix_norm[layer])
        if layer % N_MIXERS == 0:
            y = sliding_window_attention(h, positions, attn_w_qkv[j], attn_b_qkv[j],
                                         attn_sinks[j], attn_w_o[j])
        else:
            y = hgrn2_recurrence(h, lbs[layer], hgrn_w_in[j], hgrn_g_norm[j], hgrn_w_o[j])
        x = x + y.astype(x.dtype)
        h = rmsnorm(x, mlp_norm[layer])
        x = x + sqrelu_mlp(h, mlp_w_up[layer], mlp_w_down[layer]).astype(x.dtype)
    return rmsnorm(x, final_norm)
```

```python
import functools

import numpy as np
import jax
import jax.numpy as jnp
from jax import lax
from jax.experimental import pallas as pl
from jax.experimental.pallas import tpu as pltpu

D_MODEL = 1024
HEAD_DIM = 64
N_Q_HEADS = 16
N_KV_HEADS = 4
GROUP = 4
Q_DIM = N_Q_HEADS * HEAD_DIM
KV_DIM = N_KV_HEADS * HEAD_DIM
QKV_DIM = Q_DIM + 2 * KV_DIM
WINDOW = 128
BLOCK = 128
ROT_DIM = 16
ROT_HALF = ROT_DIM // 2
ROPE_THETA = 500000.0
NEG_INF = -1e30
HGRN_HEADS = 8
HGRN_DK = 128
CHUNK = 64
D_FF = 4 * D_MODEL
NORM_EPS = 1e-5

LANES = 128
VMEM_LIMIT = 56 * 1024 * 1024

BF16 = jnp.bfloat16
F32 = jnp.float32

NT_DIMS = (((1,), (1,)), ((), ()))
TN_DIMS = (((0,), (0,)), ((), ()))


def _rmsnorm(x, gain):
    ms = jnp.mean(x * x, axis=-1, keepdims=True)
    return x * lax.rsqrt(ms + NORM_EPS) * gain


def _params(*sem):
    return pltpu.CompilerParams(dimension_semantics=sem, vmem_limit_bytes=VMEM_LIMIT)


def _resident(shape):
    return pl.BlockSpec(shape, lambda *_: (0,) * len(shape))


def _rope_kernel(pos_ref, invf_ref, cos_ref, sin_ref):
    ang = pos_ref[...].astype(F32) * invf_ref[...]
    cos_ref[...] = jnp.cos(ang)
    sin_ref[...] = jnp.sin(ang)


def _rope_tables(positions):
    n_tok = positions.size
    tok_per_row = LANES // ROT_HALF
    rows = n_tok // tok_per_row
    pos_rep = jnp.repeat(positions.reshape(rows, tok_per_row), ROT_HALF, axis=1)
    inv_freq = ROPE_THETA ** (-jnp.arange(0, ROT_DIM, 2, dtype=F32) / ROT_DIM)
    invf = jnp.tile(inv_freq, tok_per_row).reshape(1, LANES)
    rb = 256
    cos, sin = pl.pallas_call(
        _rope_kernel,
        out_shape=[jax.ShapeDtypeStruct((rows, LANES), F32)] * 2,
        grid=(rows // rb,),
        in_specs=[pl.BlockSpec((rb, LANES), lambda i: (i, 0)), _resident((1, LANES))],
        out_specs=[pl.BlockSpec((rb, LANES), lambda i: (i, 0))] * 2,
        compiler_params=_params("parallel"),
        name="rope_tables",
    )(pos_rep, invf)
    return cos.reshape(n_tok, ROT_HALF), sin.reshape(n_tok, ROT_HALF)


def _qkv_kernel(x_ref, cos_ref, sin_ref, g_ref, w_ref, b_ref, q_ref, k_ref, v_ref):
    tm = x_ref.shape[0]
    h = _rmsnorm(x_ref[...], g_ref[...]).astype(BF16)
    qkv = jnp.dot(h, w_ref[...], preferred_element_type=F32) + b_ref[...]

    lane = lax.broadcasted_iota(jnp.int32, (tm, LANES), 1)
    freq = lane % ROT_HALF
    in_head = lane % HEAD_DIM
    cos8 = cos_ref[...]
    sin8 = sin_ref[...]
    cosb = jnp.zeros((tm, LANES), F32)
    sinb = jnp.zeros((tm, LANES), F32)
    for j in range(ROT_HALF):
        sel = freq == j
        cosb = jnp.where(sel, cos8[:, j:j + 1], cosb)
        sinb = jnp.where(sel, sin8[:, j:j + 1], sinb)
    c_mul = jnp.where(in_head < ROT_DIM, cosb, 1.0)
    s_lo = jnp.where(in_head < ROT_HALF, -sinb, 0.0)
    s_hi = jnp.where((in_head >= ROT_HALF) & (in_head < ROT_DIM), sinb, 0.0)

    def rotate(t):
        return (t * c_mul + pltpu.roll(t, LANES - ROT_HALF, 1) * s_lo
                + pltpu.roll(t, ROT_HALF, 1) * s_hi)

    scale = HEAD_DIM ** -0.5
    for j in range(Q_DIM // LANES):
        t = qkv[:, j * LANES:(j + 1) * LANES]
        q_ref[:, j * LANES:(j + 1) * LANES] = (rotate(t) * scale).astype(BF16)
    for j in range(KV_DIM // LANES):
        t = qkv[:, Q_DIM + j * LANES:Q_DIM + (j + 1) * LANES]
        k_ref[:, j * LANES:(j + 1) * LANES] = rotate(t).astype(BF16)
    v_ref[...] = qkv[:, Q_DIM + KV_DIM:].astype(BF16)


def _qkv_proj(x, cos8, sin8, gain, w_qkv, b_qkv, tm=512):
    n_tok = x.shape[0]
    row = lambda w: pl.BlockSpec((tm, w), lambda i: (i, 0))
    return pl.pallas_call(
        _qkv_kernel,
        out_shape=[jax.ShapeDtypeStruct((n_tok, Q_DIM), BF16),
                   jax.ShapeDtypeStruct((n_tok, KV_DIM), BF16),
                   jax.ShapeDtypeStruct((n_tok, KV_DIM), BF16)],
        grid=(n_tok // tm,),
        in_specs=[row(D_MODEL), row(ROT_HALF), row(ROT_HALF), _resident((1, D_MODEL)),
                  _resident((D_MODEL, QKV_DIM)), _resident((1, QKV_DIM))],
        out_specs=[row(Q_DIM), row(KV_DIM), row(KV_DIM)],
        compiler_params=_params("parallel"),
        name="qkv_rope",
    )(x, cos8, sin8, gain, w_qkv, b_qkv)


def _attn_kernel(sink_ref, q_ref, kp_ref, kc_ref, vp_ref, vc_ref, o_ref):
    n = pl.program_id(1)
    qi = lax.broadcasted_iota(jnp.int32, (BLOCK, 2 * BLOCK), 0)
    kj = lax.broadcasted_iota(jnp.int32, (BLOCK, 2 * BLOCK), 1)
    delta = qi + BLOCK - kj
    valid = (delta >= 0) & (delta < WINDOW) & ((kj >= BLOCK) | (n > 0))
    kband = jnp.concatenate([kp_ref[...], kc_ref[...]], axis=0)
    vband = jnp.concatenate([vp_ref[...], vc_ref[...]], axis=0)
    for kvh in range(N_KV_HEADS):
        kh = kband[:, kvh * HEAD_DIM:(kvh + 1) * HEAD_DIM]
        vh = vband[:, kvh * HEAD_DIM:(kvh + 1) * HEAD_DIM]
        for g in range(GROUP):
            head = kvh * GROUP + g
            qh = q_ref[:, head * HEAD_DIM:(head + 1) * HEAD_DIM]
            s = lax.dot_general(qh, kh, NT_DIMS, preferred_element_type=F32)
            s = jnp.where(valid, s, NEG_INF)
            sink = sink_ref[head]
            m = jnp.maximum(jnp.max(s, axis=-1, keepdims=True), sink)
            e = jnp.exp(s - m)
            denom = jnp.sum(e, axis=-1, keepdims=True) + jnp.exp(sink - m)
            pv = jnp.dot(e.astype(BF16), vh, preferred_element_type=F32)
            o_ref[:, head * HEAD_DIM:(head + 1) * HEAD_DIM] = (pv / denom).astype(BF16)


def _attention(q, k, v, sinks, bsz, seq):
    nb = seq // BLOCK
    cur = lambda b, n: (b * nb + n, 0)
    prev = lambda b, n: (b * nb + jnp.maximum(n - 1, 0), 0)
    return pl.pallas_call(
        _attn_kernel,
        out_shape=jax.ShapeDtypeStruct((bsz * seq, Q_DIM), BF16),
        grid=(bsz, nb),
        in_specs=[pl.BlockSpec(memory_space=pltpu.SMEM),
                  pl.BlockSpec((BLOCK, Q_DIM), cur),
                  pl.BlockSpec((BLOCK, KV_DIM), prev), pl.BlockSpec((BLOCK, KV_DIM), cur),
                  pl.BlockSpec((BLOCK, KV_DIM), prev), pl.BlockSpec((BLOCK, KV_DIM), cur)],
        out_specs=pl.BlockSpec((BLOCK, Q_DIM), cur),
        compiler_params=_params("parallel", "parallel"),
        name="swa_attention",
    )(sinks, q, k, k, v, v)


def _out_mlp_kernel(a_ref, x_ref, wo_ref, g_ref, wu_ref, wd_ref, gf_ref, o_ref, *, final_norm):
    x1 = x_ref[...] + jnp.dot(a_ref[...], wo_ref[...], preferred_element_type=F32)
    h = _rmsnorm(x1, g_ref[...]).astype(BF16)
    u = jnp.maximum(jnp.dot(h, wu_ref[...], preferred_element_type=F32), 0.0)
    y = jnp.dot((u * u).astype(BF16), wd_ref[...], preferred_element_type=F32)
    x2 = x1 + y
    if final_norm:
        x2 = _rmsnorm(x2, gf_ref[...])
    o_ref[...] = x2


def _out_mlp(a, x, w_o, gain, w_up, w_down, final_gain, final_norm, tm=256):
    n_tok = x.shape[0]
    row = pl.BlockSpec((tm, D_MODEL), lambda i: (i, 0))
    once = pl.Buffered(1)
    wspec = lambda shape: pl.BlockSpec(shape, lambda i: (0, 0), pipeline_mode=once)
    return pl.pallas_call(
        functools.partial(_out_mlp_kernel, final_norm=final_norm),
        out_shape=jax.ShapeDtypeStruct((n_tok, D_MODEL), F32),
        grid=(n_tok // tm,),
        in_specs=[row, row, wspec((D_MODEL, D_MODEL)), wspec((1, D_MODEL)),
                  wspec((D_MODEL, D_FF)), wspec((D_FF, D_MODEL)), wspec((1, D_MODEL))],
        out_specs=row,
        compiler_params=_params("parallel"),
        name="out_proj_mlp",
    )(a, x, w_o, gain, w_up, w_down, final_gain)


def _hgrn_in_kernel(x_ref, g_ref, w_ref, lbp_ref, q_ref, f_ref, v_ref, gate_ref):
    h = _rmsnorm(x_ref[...], g_ref[...]).astype(BF16)
    p = jnp.dot(h, w_ref[...], preferred_element_type=F32)
    lbp = lbp_ref[...]
    e = jnp.exp(lbp - jnp.max(lbp, axis=0, keepdims=True))
    sm = e / jnp.sum(e, axis=0, keepdims=True)
    lb = (sm[0:1] + sm[1:2]) - sm[0:1]
    q = p[:, :D_MODEL]
    q = q * jax.nn.sigmoid(q)
    forget = lb + (1.0 - lb) * jax.nn.sigmoid(p[:, D_MODEL:2 * D_MODEL])
    g = p[:, 3 * D_MODEL:]
    gate_ref[...] = (g * jax.nn.sigmoid(g)).astype(BF16)
    for hd in range(HGRN_HEADS):
        sl = slice(hd * HGRN_DK, (hd + 1) * HGRN_DK)
        q_ref[hd] = q[:, sl]
        f_ref[hd] = forget[:, sl]
        v_ref[hd] = p[:, 2 * D_MODEL + hd * HGRN_DK:2 * D_MODEL + (hd + 1) * HGRN_DK].astype(BF16)


def _hgrn_in(x, gain, w_in, lb_params, tm=256):
    n_tok = x.shape[0]
    head_major = pl.BlockSpec((HGRN_HEADS, tm, HGRN_DK), lambda i: (0, i, 0))
    once = pl.Buffered(1)
    return pl.pallas_call(
        _hgrn_in_kernel,
        out_shape=[jax.ShapeDtypeStruct((HGRN_HEADS, n_tok, HGRN_DK), F32),
                   jax.ShapeDtypeStruct((HGRN_HEADS, n_tok, HGRN_DK), F32),
                   jax.ShapeDtypeStruct((HGRN_HEADS, n_tok, HGRN_DK), BF16),
                   jax.ShapeDtypeStruct((n_tok, D_MODEL), BF16)],
        grid=(n_tok // tm,),
        in_specs=[pl.BlockSpec((tm, D_MODEL), lambda i: (i, 0)), _resident((1, D_MODEL)),
                  pl.BlockSpec((D_MODEL, 4 * D_MODEL), lambda i: (0, 0), pipeline_mode=once),
                  _resident((2, D_MODEL))],
        out_specs=[head_major, head_major, head_major,
                   pl.BlockSpec((tm, D_MODEL), lambda i: (i, 0))],
        compiler_params=_params("parallel"),
        name="hgrn_in_proj",
    )(x, gain, w_in, lb_params)


LEVELS = (32, 16, 8, 4, 2)


def _decay_sum_matrix():
    t = np.arange(CHUNK)[:, None]
    u = np.arange(CHUNK)[None, :]
    mats = [u <= t, u > t]
    for m in LEVELS:
        blk = t // m
        odd = (blk % 2) == 1
        q_side = (u >= blk * m) & (u <= t)
        k_side = (u > t) & (u < (blk + 1) * m)
        mats.append(np.where(odd, q_side, k_side))
    d = np.concatenate(mats, axis=0).astype(np.float32)
    return np.concatenate([d, d, d], axis=1)


def _level_index():
    t = np.arange(CHUNK)[:, None]
    s = np.arange(CHUNK)[None, :]
    x = t ^ s
    lvl = np.floor(np.log2(np.maximum(x, 1))).astype(np.int32)
    lvl = np.where(t == s, -1, lvl)
    return np.where(t < s, -2, lvl).astype(np.int32)


def _hgrn_scan_kernel(q_ref, f_ref, v_ref, dmat_ref, lvl_ref, o_ref, st_ref, *, n_chunks):
    @pl.when(pl.program_id(2) == 0)
    def _():
        st_ref[...] = jnp.zeros_like(st_ref)

    row = lax.broadcasted_iota(jnp.int32, (CHUNK, HGRN_DK), 0)
    lvl = lvl_ref[...]
    dmat = dmat_ref[...]

    def chunk(j, carry):
        rows = pl.ds(pl.multiple_of(j * CHUNK, CHUNK), CHUNK)
        qt = q_ref[0, rows, :]
        fg = f_ref[0, rows, :]
        v = v_ref[0, rows, :]
        kk = 1.0 - fg
        lf = jnp.log(fg)
        p1 = lf.astype(BF16)
        r1 = lf - p1.astype(F32)
        p2 = r1.astype(BF16)
        p3 = (r1 - p2.astype(F32)).astype(BF16)
        sums = jnp.dot(dmat, jnp.concatenate([p1, p2, p3], axis=0), preferred_element_type=F32)
        b = sums[0:CHUNK]
        st = st_ref[...]
        qe = (qt * jnp.exp(b)).astype(BF16)
        o = lax.dot_general(qe, st.astype(BF16), NT_DIMS, preferred_element_type=F32)

        scores = jnp.zeros((CHUNK, CHUNK), F32)
        for li, m in enumerate(LEVELS):
            odd = ((row // m) % 2) == 1
            z = (jnp.where(odd, qt, kk) * jnp.exp(sums[(2 + li) * CHUNK:(3 + li) * CHUNK])).astype(BF16)
            gram = lax.dot_general(z, z, NT_DIMS, preferred_element_type=F32)
            scores = jnp.where(lvl == int(np.log2(m)), gram, scores)
        z = jnp.where((row % 2) == 1, qt * fg, kk).astype(BF16)
        gram = lax.dot_general(z, z, NT_DIMS, preferred_element_type=F32)
        scores = jnp.where(lvl == 0, gram, scores)
        gram = lax.dot_general(qt.astype(BF16), kk.astype(BF16), NT_DIMS, preferred_element_type=F32)
        scores = jnp.where(lvl == -1, gram, scores)
        o = o + jnp.dot(scores.astype(BF16), v, preferred_element_type=F32)
        o_ref[0, rows, :] = o

        kd = (kk * jnp.exp(sums[CHUNK:2 * CHUNK])).astype(BF16)
        upd = lax.dot_general(v, kd, TN_DIMS, preferred_element_type=F32)
        st_ref[...] = st * jnp.exp(b[CHUNK - 1:CHUNK, :]) + upd
        return carry

    lax.fori_loop(0, n_chunks, chunk, 0)


def _hgrn_scan(q, f, v, bsz, seq, tc=512):
    n_tok = q.shape[1]
    steps = seq // tc
    blk = pl.BlockSpec((1, tc, HGRN_DK), lambda h, b, c: (h, b * steps + c, 0))
    dmat = jnp.asarray(_decay_sum_matrix(), BF16)
    lvl = jnp.asarray(_level_index())
    return pl.pallas_call(
        functools.partial(_hgrn_scan_kernel, n_chunks=tc // CHUNK),
        out_shape=jax.ShapeDtypeStruct((HGRN_HEADS, n_tok, HGRN_DK), F32),
        grid=(HGRN_HEADS, bsz, steps),
        in_specs=[blk, blk, blk, _resident(dmat.shape), _resident(lvl.shape)],
        out_specs=blk,
        scratch_shapes=[pltpu.VMEM((HGRN_DK, HGRN_DK), F32)],
        compiler_params=_params("parallel", "parallel", "arbitrary"),
        name="hgrn_scan",
    )(q, f, v, dmat, lvl)


def _hgrn_gate_kernel(o_ref, gate_ref, g_ref, a_ref):
    o = jnp.concatenate([o_ref[hd] for hd in range(HGRN_HEADS)], axis=1)
    a_ref[...] = (_rmsnorm(o, g_ref[...]) * gate_ref[...].astype(F32)).astype(BF16)


def _hgrn_gate(o, gate, g_norm, tm=512):
    n_tok = gate.shape[0]
    row = pl.BlockSpec((tm, D_MODEL), lambda i: (i, 0))
    return pl.pallas_call(
        _hgrn_gate_kernel,
        out_shape=jax.ShapeDtypeStruct((n_tok, D_MODEL), BF16),
        grid=(n_tok // tm,),
        in_specs=[pl.BlockSpec((HGRN_HEADS, tm, HGRN_DK), lambda i: (0, i, 0)), row,
                  _resident((1, D_MODEL))],
        out_specs=row,
        compiler_params=_params("parallel"),
        name="hgrn_gate",
    )(o, gate, g_norm)


def kernel(x, positions, mix_norm, mlp_norm, final_norm, attn_w_qkv, attn_b_qkv, attn_sinks,
           attn_w_o, hgrn_w_in, hgrn_g_norm, hgrn_w_o, hgrn_lower_bounds, mlp_w_up, mlp_w_down):
    bsz, seq, _ = x.shape
    n_tok = bsz * seq
    xf = x.reshape(n_tok, D_MODEL)
    gain = lambda g: g.reshape(1, D_MODEL).astype(F32)

    cos8, sin8 = _rope_tables(positions)
    q, k, v = _qkv_proj(xf, cos8, sin8, gain(mix_norm[0]), attn_w_qkv[0].astype(BF16),
                        attn_b_qkv[0].reshape(1, QKV_DIM).astype(F32))
    a = _attention(q, k, v, attn_sinks[0].astype(F32), bsz, seq)
    xf = _out_mlp(a, xf, attn_w_o[0].astype(BF16), gain(mlp_norm[0]), mlp_w_up[0].astype(BF16),
                  mlp_w_down[0].astype(BF16), gain(final_norm), final_norm=False)

    hq, hf, hv, gate = _hgrn_in(xf, gain(mix_norm[1]), hgrn_w_in[0].astype(BF16),
                                hgrn_lower_bounds.astype(F32))
    o = _hgrn_scan(hq, hf, hv, bsz, seq)
    a = _hgrn_gate(o, gate, gain(hgrn_g_norm[0]))
    xf = _out_mlp(a, xf, hgrn_w_o[0].astype(BF16), gain(mlp_norm[1]), mlp_w_up[1].astype(BF16),
                  mlp_w_down[1].astype(BF16), gain(final_norm), final_norm=True)
    return xf.reshape(bsz, seq, D_MODEL)
```

```python
import functools

import numpy as np
import jax
import jax.numpy as jnp
from jax import lax
from jax.experimental import pallas as pl
from jax.experimental.pallas import tpu as pltpu

D_MODEL = 1024
HEAD_DIM = 64
N_Q_HEADS = 16
N_KV_HEADS = 4
GROUP = 4
Q_DIM = N_Q_HEADS * HEAD_DIM
KV_DIM = N_KV_HEADS * HEAD_DIM
QKV_DIM = Q_DIM + 2 * KV_DIM
WINDOW = 128
BLOCK = 128
ROT_DIM = 16
ROT_HALF = ROT_DIM // 2
ROPE_THETA = 500000.0
NEG_INF = -1e30
HGRN_HEADS = 8
HGRN_DK = 128
CHUNK = 64
D_FF = 4 * D_MODEL
NORM_EPS = 1e-5

LANES = 128
VMEM_LIMIT = 56 * 1024 * 1024

BF16 = jnp.bfloat16
F32 = jnp.float32

NT_DIMS = (((1,), (1,)), ((), ()))
TN_DIMS = (((0,), (0,)), ((), ()))


def _rmsnorm(x, gain):
    ms = jnp.mean(x * x, axis=-1, keepdims=True)
    return x * lax.rsqrt(ms + NORM_EPS) * gain


def _params(*sem):
    return pltpu.CompilerParams(dimension_semantics=sem, vmem_limit_bytes=VMEM_LIMIT)


def _resident(shape):
    return pl.BlockSpec(shape, lambda *_: (0,) * len(shape))


def _rope_kernel(pos_ref, invf_ref, cos_ref, sin_ref):
    ang = pos_ref[...].astype(F32) * invf_ref[...]
    cos_ref[...] = jnp.cos(ang)
    sin_ref[...] = jnp.sin(ang)


def _rope_tables(positions):
    n_tok = positions.size
    tok_per_row = LANES // ROT_HALF
    rows = n_tok // tok_per_row
    pos_rep = jnp.repeat(positions.reshape(rows, tok_per_row), ROT_HALF, axis=1)
    inv_freq = ROPE_THETA ** (-jnp.arange(0, ROT_DIM, 2, dtype=F32) / ROT_DIM)
    invf = jnp.tile(inv_freq, tok_per_row).reshape(1, LANES)
    rb = 256
    cos, sin = pl.pallas_call(
        _rope_kernel,
        out_shape=[jax.ShapeDtypeStruct((rows, LANES), F32)] * 2,
        grid=(rows // rb,),
        in_specs=[pl.BlockSpec((rb, LANES), lambda i: (i, 0)), _resident((1, LANES))],
        out_specs=[pl.BlockSpec((rb, LANES), lambda i: (i, 0))] * 2,
        compiler_params=_params("parallel"),
        name="rope_tables",
    )(pos_rep, invf)
    return cos.reshape(n_tok, ROT_HALF), sin.reshape(n_tok, ROT_HALF)


def _qkv_kernel(x_ref, cos_ref, sin_ref, g_ref, w_ref, b_ref, q_ref, k_ref, v_ref):
    tm = x_ref.shape[0]
    h = _rmsnorm(x_ref[...], g_ref[...]).astype(BF16)
    qkv = jnp.dot(h, w_ref[...], preferred_element_type=F32) + b_ref[...]

    lane = lax.broadcasted_iota(jnp.int32, (tm, LANES), 1)
    freq = lane % ROT_HALF
    in_head = lane % HEAD_DIM
    cos8 = cos_ref[...]
    sin8 = sin_ref[...]
    cosb = jnp.zeros((tm, LANES), F32)
    sinb = jnp.zeros((tm, LANES), F32)
    for j in range(ROT_HALF):
        sel = freq == j
        cosb = jnp.where(sel, cos8[:, j:j + 1], cosb)
        sinb = jnp.where(sel, sin8[:, j:j + 1], sinb)
    c_mul = jnp.where(in_head < ROT_DIM, cosb, 1.0)
    s_lo = jnp.where(in_head < ROT_HALF, -sinb, 0.0)
    s_hi = jnp.where((in_head >= ROT_HALF) & (in_head < ROT_DIM), sinb, 0.0)

    def rotate(t):
        return (t * c_mul + pltpu.roll(t, LANES - ROT_HALF, 1) * s_lo
                + pltpu.roll(t, ROT_HALF, 1) * s_hi)

    scale = HEAD_DIM ** -0.5
    for j in range(Q_DIM // LANES):
        t = qkv[:, j * LANES:(j + 1) * LANES]
        q_ref[:, j * LANES:(j + 1) * LANES] = (rotate(t) * scale).astype(BF16)
    for j in range(KV_DIM // LANES):
        t = qkv[:, Q_DIM + j * LANES:Q_DIM + (j + 1) * LANES]
        k_ref[:, j * LANES:(j + 1) * LANES] = rotate(t).astype(BF16)
    v_ref[...] = qkv[:, Q_DIM + KV_DIM:].astype(BF16)


def _qkv_proj(x, cos8, sin8, gain, w_qkv, b_qkv, tm=512):
    n_tok = x.shape[0]
    row = lambda w: pl.BlockSpec((tm, w), lambda i: (i, 0))
    return pl.pallas_call(
        _qkv_kernel,
        out_shape=[jax.ShapeDtypeStruct((n_tok, Q_DIM), BF16),
                   jax.ShapeDtypeStruct((n_tok, KV_DIM), BF16),
                   jax.ShapeDtypeStruct((n_tok, KV_DIM), BF16)],
        grid=(n_tok // tm,),
        in_specs=[row(D_MODEL), row(ROT_HALF), row(ROT_HALF), _resident((1, D_MODEL)),
                  _resident((D_MODEL, QKV_DIM)), _resident((1, QKV_DIM))],
        out_specs=[row(Q_DIM), row(KV_DIM), row(KV_DIM)],
        compiler_params=_params("parallel"),
        name="qkv_rope",
    )(x, cos8, sin8, gain, w_qkv, b_qkv)


def _attn_kernel(sink_ref, q_ref, kp_ref, kc_ref, vp_ref, vc_ref, o_ref):
    n = pl.program_id(1)
    qi = lax.broadcasted_iota(jnp.int32, (BLOCK, 2 * BLOCK), 0)
    kj = lax.broadcasted_iota(jnp.int32, (BLOCK, 2 * BLOCK), 1)
    delta = qi + BLOCK - kj
    valid = (delta >= 0) & (delta < WINDOW) & ((kj >= BLOCK) | (n > 0))
    kband = jnp.concatenate([kp_ref[...], kc_ref[...]], axis=0)
    vband = jnp.concatenate([vp_ref[...], vc_ref[...]], axis=0)
    for kvh in range(N_KV_HEADS):
        kh = kband[:, kvh * HEAD_DIM:(kvh + 1) * HEAD_DIM]
        vh = vband[:, kvh * HEAD_DIM:(kvh + 1) * HEAD_DIM]
        for g in range(GROUP):
            head = kvh * GROUP + g
            qh = q_ref[:, head * HEAD_DIM:(head + 1) * HEAD_DIM]
            s = lax.dot_general(qh, kh, NT_DIMS, preferred_element_type=F32)
            s = jnp.where(valid, s, NEG_INF)
            sink = sink_ref[head]
            m = jnp.maximum(jnp.max(s, axis=-1, keepdims=True), sink)
            e = jnp.exp(s - m)
            denom = jnp.sum(e, axis=-1, keepdims=True) + jnp.exp(sink - m)
            pv = jnp.dot(e.astype(BF16), vh, preferred_element_type=F32)
            o_ref[:, head * HEAD_DIM:(head + 1) * HEAD_DIM] = (pv / denom).astype(BF16)


def _attention(q, k, v, sinks, bsz, seq):
    nb = seq // BLOCK
    cur = lambda b, n: (b * nb + n, 0)
    prev = lambda b, n: (b * nb + jnp.maximum(n - 1, 0), 0)
    return pl.pallas_call(
        _attn_kernel,
        out_shape=jax.ShapeDtypeStruct((bsz * seq, Q_DIM), BF16),
        grid=(bsz, nb),
        in_specs=[pl.BlockSpec(memory_space=pltpu.SMEM),
                  pl.BlockSpec((BLOCK, Q_DIM), cur),
                  pl.BlockSpec((BLOCK, KV_DIM), prev), pl.BlockSpec((BLOCK, KV_DIM), cur),
                  pl.BlockSpec((BLOCK, KV_DIM), prev), pl.BlockSpec((BLOCK, KV_DIM), cur)],
        out_specs=pl.BlockSpec((BLOCK, Q_DIM), cur),
        compiler_params=_params("parallel", "parallel"),
        name="swa_attention",
    )(sinks, q, k, k, v, v)


def _out_mlp_kernel(a_ref, x_ref, wo_ref, g_ref, wu_ref, wd_ref, gf_ref, o_ref, *, final_norm):
    x1 = x_ref[...] + jnp.dot(a_ref[...], wo_ref[...], preferred_element_type=F32)
    h = _rmsnorm(x1, g_ref[...]).astype(BF16)
    u = jnp.maximum(jnp.dot(h, wu_ref[...], preferred_element_type=F32), 0.0)
    y = jnp.dot((u * u).astype(BF16), wd_ref[...], preferred_element_type=F32)
    x2 = x1 + y
    if final_norm:
        x2 = _rmsnorm(x2, gf_ref[...])
    o_ref[...] = x2


def _out_mlp(a, x, w_o, gain, w_up, w_down, final_gain, final_norm, tm=256):
    n_tok = x.shape[0]
    row = pl.BlockSpec((tm, D_MODEL), lambda i: (i, 0))
    once = pl.Buffered(1)
    wspec = lambda shape: pl.BlockSpec(shape, lambda i: (0, 0), pipeline_mode=once)
    return pl.pallas_call(
        functools.partial(_out_mlp_kernel, final_norm=final_norm),
        out_shape=jax.ShapeDtypeStruct((n_tok, D_MODEL), F32),
        grid=(n_tok // tm,),
        in_specs=[row, row, wspec((D_MODEL, D_MODEL)), wspec((1, D_MODEL)),
                  wspec((D_MODEL, D_FF)), wspec((D_FF, D_MODEL)), wspec((1, D_MODEL))],
        out_specs=row,
        compiler_params=_params("parallel"),
        name="out_proj_mlp",
    )(a, x, w_o, gain, w_up, w_down, final_gain)


def _hgrn_in_kernel(x_ref, g_ref, w_ref, lbp_ref, q_ref, f_ref, v_ref, gate_ref):
    h = _rmsnorm(x_ref[...], g_ref[...]).astype(BF16)
    p = jnp.dot(h, w_ref[...], preferred_element_type=F32)
    lbp = lbp_ref[...]
    e = jnp.exp(lbp - jnp.max(lbp, axis=0, keepdims=True))
    sm = e / jnp.sum(e, axis=0, keepdims=True)
    lb = (sm[0:1] + sm[1:2]) - sm[0:1]
    q = p[:, :D_MODEL]
    q = q * jax.nn.sigmoid(q)
    forget = lb + (1.0 - lb) * jax.nn.sigmoid(p[:, D_MODEL:2 * D_MODEL])
    g = p[:, 3 * D_MODEL:]
    gate_ref[...] = (g * jax.nn.sigmoid(g)).astype(BF16)
    for hd in range(HGRN_HEADS):
        sl = slice(hd * HGRN_DK, (hd + 1) * HGRN_DK)
        q_ref[hd] = q[:, sl]
        f_ref[hd] = forget[:, sl]
        v_ref[hd] = p[:, 2 * D_MODEL + hd * HGRN_DK:2 * D_MODEL + (hd + 1) * HGRN_DK].astype(BF16)


def _hgrn_in(x, gain, w_in, lb_params, tm=256):
    n_tok = x.shape[0]
    head_major = pl.BlockSpec((HGRN_HEADS, tm, HGRN_DK), lambda i: (0, i, 0))
    once = pl.Buffered(1)
    return pl.pallas_call(
        _hgrn_in_kernel,
        out_shape=[jax.ShapeDtypeStruct((HGRN_HEADS, n_tok, HGRN_DK), F32),
                   jax.ShapeDtypeStruct((HGRN_HEADS, n_tok, HGRN_DK), F32),
                   jax.ShapeDtypeStruct((HGRN_HEADS, n_tok, HGRN_DK), BF16),
                   jax.ShapeDtypeStruct((n_tok, D_MODEL), BF16)],
        grid=(n_tok // tm,),
        in_specs=[pl.BlockSpec((tm, D_MODEL), lambda i: (i, 0)), _resident((1, D_MODEL)),
                  pl.BlockSpec((D_MODEL, 4 * D_MODEL), lambda i: (0, 0), pipeline_mode=once),
                  _resident((2, D_MODEL))],
        out_specs=[head_major, head_major, head_major,
                   pl.BlockSpec((tm, D_MODEL), lambda i: (i, 0))],
        compiler_params=_params("parallel"),
        name="hgrn_in_proj",
    )(x, gain, w_in, lb_params)


LEVELS = (32, 16, 8, 4, 2)


def _decay_sum_matrix():
    t = np.arange(CHUNK)[:, None]
    u = np.arange(CHUNK)[None, :]
    mats = [u <= t, u > t]
    for m in LEVELS:
        blk = t // m
        odd = (blk % 2) == 1
        q_side = (u >= blk * m) & (u <= t)
        k_side = (u > t) & (u < (blk + 1) * m)
        mats.append(np.where(odd, q_side, k_side))
    d = np.concatenate(mats, axis=0).astype(np.float32)
    return np.concatenate([d, d, d], axis=1)


def _level_index():
    t = np.arange(CHUNK)[:, None]
    s = np.arange(CHUNK)[None, :]
    x = t ^ s
    lvl = np.floor(np.log2(np.maximum(x, 1))).astype(np.int32)
    lvl = np.where(t == s, -1, lvl)
    return np.where(t < s, -2, lvl).astype(np.int32)


def _hgrn_scan_kernel(q_ref, f_ref, v_ref, gate_ref, gn_ref, dmat_ref, lvl_ref, a_ref, st_ref,
                      *, n_chunks):
    @pl.when(pl.program_id(1) == 0)
    def _():
        st_ref[...] = jnp.zeros_like(st_ref)

    pair_w = 2 * HGRN_DK
    row = lax.broadcasted_iota(jnp.int32, (CHUNK, pair_w), 0)
    lvl = lvl_ref[...]
    dmat = dmat_ref[...]
    zero_c = jnp.zeros((CHUNK, HGRN_DK), BF16)
    zero_s = jnp.zeros((HGRN_DK, HGRN_DK), BF16)

    def block_diag(a0, a1, zero):
        return jnp.concatenate([jnp.concatenate([a0, zero], axis=1),
                                jnp.concatenate([zero, a1], axis=1)], axis=0)

    pairs = range(HGRN_HEADS // 2)

    def gram(z):
        zb = block_diag(z[:, :HGRN_DK], z[:, HGRN_DK:], zero_c)
        return lax.dot_general(z, zb, NT_DIMS, preferred_element_type=F32)

    def load_and_sum(j):
        rows = slice(j * CHUNK, (j + 1) * CHUNK)
        out = []
        for p in pairs:
            cat = lambda ref: jnp.concatenate([ref[2 * p, rows, :], ref[2 * p + 1, rows, :]], axis=1)
            qt, fg, v = cat(q_ref), cat(f_ref), cat(v_ref)
            lf = jnp.log(fg)
            p1 = lf.astype(BF16)
            r1 = lf - p1.astype(F32)
            p2 = r1.astype(BF16)
            p3 = (r1 - p2.astype(F32)).astype(BF16)
            sums = jnp.dot(dmat, jnp.concatenate([p1, p2, p3], axis=0), preferred_element_type=F32)
            out.append((qt, fg, v, 1.0 - fg, sums))
        return out

    def scores_and_inter(staged):
        out = []
        for p in pairs:
            qt, fg, _, kk, sums = staged[p]
            qe = (qt * jnp.exp(sums[0:CHUNK])).astype(BF16)
            st_bd = block_diag(st_ref[2 * p].astype(BF16), st_ref[2 * p + 1].astype(BF16), zero_s)
            o_inter = lax.dot_general(qe, st_bd, NT_DIMS, preferred_element_type=F32)
            scores = jnp.zeros((CHUNK, 2 * CHUNK), F32)
            for li, m in enumerate(LEVELS):
                odd = ((row // m) % 2) == 1
                z = (jnp.where(odd, qt, kk) * jnp.exp(sums[(2 + li) * CHUNK:(3 + li) * CHUNK]))
                scores = jnp.where(lvl == int(np.log2(m)), gram(z.astype(BF16)), scores)
            z = jnp.where((row % 2) == 1, qt * fg, kk).astype(BF16)
            scores = jnp.where(lvl == 0, gram(z), scores)
            kkb = kk.astype(BF16)
            diag = lax.dot_general(qt.astype(BF16), block_diag(kkb[:, :HGRN_DK], kkb[:, HGRN_DK:], zero_c),
                                   NT_DIMS, preferred_element_type=F32)
            out.append((o_inter, jnp.where(lvl == -1, diag, scores)))
        return out

    def output_and_state(j, staged, scored):
        rows = slice(j * CHUNK, (j + 1) * CHUNK)
        outs = []
        for p in pairs:
            _, _, v, kk, sums = staged[p]
            o_inter, scores = scored[p]
            outs.append(o_inter + jnp.dot(scores.astype(BF16),
                                          block_diag(v[:, :HGRN_DK], v[:, HGRN_DK:], zero_c),
                                          preferred_element_type=F32))
            kd = (kk * jnp.exp(sums[CHUNK:2 * CHUNK])).astype(BF16)
            dec = jnp.exp(sums[CHUNK - 1:CHUNK, :])
            for hh, sl in ((2 * p, slice(0, HGRN_DK)), (2 * p + 1, slice(HGRN_DK, pair_w))):
                upd = lax.dot_general(v[:, sl], kd[:, sl], TN_DIMS, preferred_element_type=F32)
                st_ref[hh] = st_ref[hh] * dec[:, sl] + upd
        o_all = jnp.concatenate(outs, axis=1)
        a_ref[rows, :] = (_rmsnorm(o_all, gn_ref[...]) * gate_ref[rows, :].astype(F32)).astype(BF16)

    staged = load_and_sum(0)
    for j in range(n_chunks):
        scored = scores_and_inter(staged)
        nxt = load_and_sum(j + 1) if j + 1 < n_chunks else None
        output_and_state(j, staged, scored)
        staged = nxt


def _hgrn_scan(q, f, v, gate, g_norm, bsz, seq, tc=256):
    n_tok = q.shape[1]
    steps = seq // tc
    heads = pl.BlockSpec((HGRN_HEADS, tc, HGRN_DK), lambda b, c: (0, b * steps + c, 0))
    toks = pl.BlockSpec((tc, D_MODEL), lambda b, c: (b * steps + c, 0))
    dmat = jnp.asarray(_decay_sum_matrix(), BF16)
    lvl = jnp.asarray(np.tile(_level_index(), (1, 2)))
    return pl.pallas_call(
        functools.partial(_hgrn_scan_kernel, n_chunks=tc // CHUNK),
        out_shape=jax.ShapeDtypeStruct((n_tok, D_MODEL), BF16),
        grid=(bsz, steps),
        in_specs=[heads, heads, heads, toks, _resident((1, D_MODEL)), _resident(dmat.shape),
                  _resident(lvl.shape)],
        out_specs=toks,
        scratch_shapes=[pltpu.VMEM((HGRN_HEADS, HGRN_DK, HGRN_DK), F32)],
        compiler_params=_params("parallel", "arbitrary"),
        name="hgrn_scan",
    )(q, f, v, gate, g_norm, dmat, lvl)


def kernel(x, positions, mix_norm, mlp_norm, final_norm, attn_w_qkv, attn_b_qkv, attn_sinks,
           attn_w_o, hgrn_w_in, hgrn_g_norm, hgrn_w_o, hgrn_lower_bounds, mlp_w_up, mlp_w_down):
    bsz, seq, _ = x.shape
    n_tok = bsz * seq
    xf = x.reshape(n_tok, D_MODEL)
    gain = lambda g: g.reshape(1, D_MODEL).astype(F32)

    cos8, sin8 = _rope_tables(positions)
    q, k, v = _qkv_proj(xf, cos8, sin8, gain(mix_norm[0]), attn_w_qkv[0].astype(BF16),
                        attn_b_qkv[0].reshape(1, QKV_DIM).astype(F32))
    a = _attention(q, k, v, attn_sinks[0].astype(F32), bsz, seq)
    xf = _out_mlp(a, xf, attn_w_o[0].astype(BF16), gain(mlp_norm[0]), mlp_w_up[0].astype(BF16),
                  mlp_w_down[0].astype(BF16), gain(final_norm), final_norm=False)

    hq, hf, hv, gate = _hgrn_in(xf, gain(mix_norm[1]), hgrn_w_in[0].astype(BF16),
                                hgrn_lower_bounds.astype(F32))
    a = _hgrn_scan(hq, hf, hv, gate, gain(hgrn_g_norm[0]), bsz, seq)
    xf = _out_mlp(a, xf, hgrn_w_o[0].astype(BF16), gain(mlp_norm[1]), mlp_w_up[1].astype(BF16),
                  mlp_w_down[1].astype(BF16), gain(final_norm), final_norm=True)
    return xf.reshape(bsz, seq, D_MODEL)
```

```python
import functools

import numpy as np
import jax
import jax.numpy as jnp
from jax import lax
from jax.experimental import pallas as pl
from jax.experimental.pallas import tpu as pltpu

D_MODEL = 1024
HEAD_DIM = 64
N_Q_HEADS = 16
N_KV_HEADS = 4
GROUP = 4
Q_DIM = N_Q_HEADS * HEAD_DIM
KV_DIM = N_KV_HEADS * HEAD_DIM
QKV_DIM = Q_DIM + 2 * KV_DIM
WINDOW = 128
BLOCK = 128
ROT_DIM = 16
ROT_HALF = ROT_DIM // 2
ROPE_THETA = 500000.0
NEG_INF = -1e30
HGRN_HEADS = 8
HGRN_DK = 128
CHUNK = 64
D_FF = 4 * D_MODEL
NORM_EPS = 1e-5
LOG2E = 1.4426950408889634

LANES = 128
VMEM_LIMIT = 56 * 1024 * 1024

BF16 = jnp.bfloat16
F32 = jnp.float32

NT_DIMS = (((1,), (1,)), ((), ()))
TN_DIMS = (((0,), (0,)), ((), ()))


def _rmsnorm(x, gain):
    ms = jnp.mean(x * x, axis=-1, keepdims=True)
    return x * lax.rsqrt(ms + NORM_EPS) * gain


def _params(*sem):
    return pltpu.CompilerParams(dimension_semantics=sem, vmem_limit_bytes=VMEM_LIMIT)


def _resident(shape):
    return pl.BlockSpec(shape, lambda *_: (0,) * len(shape))


def _rope_kernel(pos_ref, invf_ref, cos_ref, sin_ref):
    ang = pos_ref[...].astype(F32) * invf_ref[...]
    cos_ref[...] = jnp.cos(ang)
    sin_ref[...] = jnp.sin(ang)


def _rope_tables(positions, tb=2048):
    n_tok = positions.size
    inv_freq = ROPE_THETA ** (-jnp.arange(0, ROT_DIM, 2, dtype=F32) / ROT_DIM)
    table = pl.BlockSpec((ROT_HALF, tb), lambda i: (0, i))
    return pl.pallas_call(
        _rope_kernel,
        out_shape=[jax.ShapeDtypeStruct((ROT_HALF, n_tok), F32)] * 2,
        grid=(n_tok // tb,),
        in_specs=[pl.BlockSpec((1, tb), lambda i: (0, i)), _resident((ROT_HALF, 1))],
        out_specs=[table, table],
        compiler_params=_params("parallel"),
        name="rope_tables",
    )(positions.reshape(1, n_tok), inv_freq.reshape(ROT_HALF, 1))


def _qkv_kernel(x_ref, cos_ref, sin_ref, g_ref, w_ref, b_ref, qt_ref, k_ref, vt_ref):
    h = _rmsnorm(x_ref[...], g_ref[...]).astype(BF16)
    qkv = jnp.dot(h, w_ref[...], preferred_element_type=F32) + b_ref[...]
    cos_t = cos_ref[...]
    sin_t = sin_ref[...]

    def rotate_t(tt):
        parts = []
        for hd in range(LANES // HEAD_DIM):
            base = hd * HEAD_DIM
            x1 = tt[base:base + ROT_HALF]
            x2 = tt[base + ROT_HALF:base + ROT_DIM]
            parts += [x1 * cos_t - x2 * sin_t, x2 * cos_t + x1 * sin_t,
                      tt[base + ROT_DIM:base + HEAD_DIM]]
        return jnp.concatenate(parts, axis=0)

    scale = HEAD_DIM ** -0.5 * LOG2E
    for j in range(Q_DIM // LANES):
        t = qkv[:, j * LANES:(j + 1) * LANES] * scale
        qt_ref[j * LANES:(j + 1) * LANES, :] = rotate_t(t.T).astype(BF16)
    for j in range(KV_DIM // LANES):
        t = qkv[:, Q_DIM + j * LANES:Q_DIM + (j + 1) * LANES]
        k_ref[:, j * LANES:(j + 1) * LANES] = rotate_t(t.T).T.astype(BF16)
        t = qkv[:, Q_DIM + KV_DIM + j * LANES:Q_DIM + KV_DIM + (j + 1) * LANES]
        vt_ref[j * LANES:(j + 1) * LANES, :] = t.T.astype(BF16)


def _qkv_proj(x, cos_t, sin_t, gain, w_qkv, b_qkv, tm=512):
    n_tok = x.shape[0]
    row = lambda w: pl.BlockSpec((tm, w), lambda i: (i, 0))
    col = lambda h: pl.BlockSpec((h, tm), lambda i: (0, i))
    return pl.pallas_call(
        _qkv_kernel,
        out_shape=[jax.ShapeDtypeStruct((Q_DIM, n_tok), BF16),
                   jax.ShapeDtypeStruct((n_tok, KV_DIM), BF16),
                   jax.ShapeDtypeStruct((KV_DIM, n_tok), BF16)],
        grid=(n_tok // tm,),
        in_specs=[row(D_MODEL), col(ROT_HALF), col(ROT_HALF), _resident((1, D_MODEL)),
                  _resident((D_MODEL, QKV_DIM)), _resident((1, QKV_DIM))],
        out_specs=[col(Q_DIM), row(KV_DIM), col(KV_DIM)],
        compiler_params=_params("parallel"),
        name="qkv_rope",
    )(x, cos_t, sin_t, gain, w_qkv, b_qkv)


def _attn_kernel(sink_ref, qt_ref, kp_ref, kc_ref, vtp_ref, vtc_ref, o_ref, *, q_blocks):
    n = pl.program_id(1)
    band = 2 * BLOCK
    width = GROUP * BLOCK
    kj = lax.broadcasted_iota(jnp.int32, (band, width), 0)
    qi = lax.broadcasted_iota(jnp.int32, (band, width), 1) % BLOCK
    delta = qi + BLOCK - kj
    in_window = (delta >= 0) & (delta < WINDOW)
    valid_first = in_window & ((kj >= BLOCK) | (n > 0))
    head_of_lane = lax.broadcasted_iota(jnp.int32, (1, width), 1) // BLOCK
    zero_q = jnp.zeros((HEAD_DIM, width), BF16)
    ones_rows = jnp.ones((16, band), BF16)

    def scores(j, kvh):
        tile = slice((kvh // 2) * LANES, (kvh // 2 + 1) * LANES)
        if j == 0:
            kband = jnp.concatenate([kp_ref[:, tile], kc_ref[:BLOCK, tile]], axis=0)
        else:
            kband = kc_ref[(j - 1) * BLOCK:(j + 1) * BLOCK, tile]
        qc = slice(j * BLOCK, (j + 1) * BLOCK)
        qt4 = jnp.concatenate(
            [qt_ref[(GROUP * kvh + g) * HEAD_DIM:(GROUP * kvh + g + 1) * HEAD_DIM, qc]
             for g in range(GROUP)], axis=1)
        rhs = jnp.concatenate([qt4, zero_q] if kvh % 2 == 0 else [zero_q, qt4], axis=0)
        s = jnp.dot(kband, rhs, preferred_element_type=F32)
        return jnp.where(valid_first if j == 0 else in_window, s, NEG_INF)

    def softmax_pv(j, kvh, s):
        hs = slice(kvh * HEAD_DIM, (kvh + 1) * HEAD_DIM)
        if j == 0:
            vt = jnp.concatenate([vtp_ref[hs, :], vtc_ref[hs, :BLOCK]], axis=1)
        else:
            vt = vtc_ref[hs, (j - 1) * BLOCK:(j + 1) * BLOCK]
        sink = jnp.zeros((1, width), F32)
        for g in range(GROUP):
            sink = jnp.where(head_of_lane == g, sink_ref[GROUP * kvh + g] * LOG2E, sink)
        m = jnp.maximum(jnp.max(s, axis=0, keepdims=True), sink)
        e = jnp.exp2(s - m).astype(BF16)
        pv = jnp.dot(jnp.concatenate([vt, ones_rows], axis=0), e,
                     preferred_element_type=F32)
        den = pv[HEAD_DIM:HEAD_DIM + 1, :] + jnp.exp2(sink - m)
        out_t = pv[:HEAD_DIM, :] * (1.0 / den)
        qr = slice(j * BLOCK, (j + 1) * BLOCK)
        for t in range(2):
            pair = jnp.concatenate([out_t[:, (2 * t) * BLOCK:(2 * t + 1) * BLOCK],
                                    out_t[:, (2 * t + 1) * BLOCK:(2 * t + 2) * BLOCK]], axis=0)
            o_ref[qr, (2 * kvh + t) * LANES:(2 * kvh + t + 1) * LANES] = pair.T.astype(BF16)

    items = [(j, kvh) for j in range(q_blocks) for kvh in range(N_KV_HEADS)]
    s = scores(*items[0])
    for i, item in enumerate(items):
        s_next = scores(*items[i + 1]) if i + 1 < len(items) else None
        softmax_pv(*item, s)
        s = s_next


def _attention(qt, k, vt, sinks, bsz, seq, q_blocks=2):
    tq = q_blocks * BLOCK
    steps = seq // tq
    nb = seq // BLOCK
    cur = lambda b, n: (b * steps + n, 0)
    prev = lambda b, n: (b * nb + jnp.maximum(n * q_blocks - 1, 0), 0)
    cur_t = lambda b, n: (0, b * steps + n)
    prev_t = lambda b, n: (0, b * nb + jnp.maximum(n * q_blocks - 1, 0))
    return pl.pallas_call(
        functools.partial(_attn_kernel, q_blocks=q_blocks),
        out_shape=jax.ShapeDtypeStruct((bsz * seq, Q_DIM), BF16),
        grid=(bsz, steps),
        in_specs=[pl.BlockSpec(memory_space=pltpu.SMEM),
                  pl.BlockSpec((Q_DIM, tq), cur_t),
                  pl.BlockSpec((BLOCK, KV_DIM), prev), pl.BlockSpec((tq, KV_DIM), cur),
                  pl.BlockSpec((KV_DIM, BLOCK), prev_t), pl.BlockSpec((KV_DIM, tq), cur_t)],
        out_specs=pl.BlockSpec((tq, Q_DIM), cur),
        compiler_params=_params("parallel", "parallel"),
        name="swa_attention",
    )(sinks, qt, k, k, vt, vt)


def _out_mlp_kernel(a_ref, x_ref, wo_ref, g_ref, wu_ref, wd_ref, gf_ref, o_ref, *, final_norm):
    x1 = x_ref[...] + jnp.dot(a_ref[...], wo_ref[...], preferred_element_type=F32)
    h = _rmsnorm(x1, g_ref[...]).astype(BF16)
    u = jnp.maximum(jnp.dot(h, wu_ref[...], preferred_element_type=F32), 0.0)
    y = jnp.dot((u * u).astype(BF16), wd_ref[...], preferred_element_type=F32)
    x2 = x1 + y
    if final_norm:
        x2 = _rmsnorm(x2, gf_ref[...])
    o_ref[...] = x2


def _out_mlp(a, x, w_o, gain, w_up, w_down, final_gain, final_norm, tm=256):
    n_tok = x.shape[0]
    row = pl.BlockSpec((tm, D_MODEL), lambda i: (i, 0))
    once = pl.Buffered(1)
    wspec = lambda shape: pl.BlockSpec(shape, lambda i: (0, 0), pipeline_mode=once)
    return pl.pallas_call(
        functools.partial(_out_mlp_kernel, final_norm=final_norm),
        out_shape=jax.ShapeDtypeStruct((n_tok, D_MODEL), F32),
        grid=(n_tok // tm,),
        in_specs=[row, row, wspec((D_MODEL, D_MODEL)), wspec((1, D_MODEL)),
                  wspec((D_MODEL, D_FF)), wspec((D_FF, D_MODEL)), wspec((1, D_MODEL))],
        out_specs=row,
        compiler_params=_params("parallel"),
        name="out_proj_mlp",
    )(a, x, w_o, gain, w_up, w_down, final_gain)


def _hgrn_in_kernel(x_ref, g_ref, w_ref, lbp_ref, q_ref, f_ref, v_ref, gate_ref):
    h = _rmsnorm(x_ref[...], g_ref[...]).astype(BF16)
    p = jnp.dot(h, w_ref[...], preferred_element_type=F32)
    lbp = lbp_ref[...]
    e = jnp.exp(lbp - jnp.max(lbp, axis=0, keepdims=True))
    sm = e / jnp.sum(e, axis=0, keepdims=True)
    lb = (sm[0:1] + sm[1:2]) - sm[0:1]
    q = p[:, :D_MODEL]
    q = q * jax.nn.sigmoid(q)
    forget = lb + (1.0 - lb) * jax.nn.sigmoid(p[:, D_MODEL:2 * D_MODEL])
    g = p[:, 3 * D_MODEL:]
    gate_ref[...] = (g * jax.nn.sigmoid(g)).astype(BF16)
    for hd in range(HGRN_HEADS):
        sl = slice(hd * HGRN_DK, (hd + 1) * HGRN_DK)
        q_ref[hd] = q[:, sl]
        f_ref[hd] = forget[:, sl]
        v_ref[hd] = p[:, 2 * D_MODEL + hd * HGRN_DK:2 * D_MODEL + (hd + 1) * HGRN_DK].astype(BF16)


def _hgrn_in(x, gain, w_in, lb_params, tm=256):
    n_tok = x.shape[0]
    head_major = pl.BlockSpec((HGRN_HEADS, tm, HGRN_DK), lambda i: (0, i, 0))
    once = pl.Buffered(1)
    return pl.pallas_call(
        _hgrn_in_kernel,
        out_shape=[jax.ShapeDtypeStruct((HGRN_HEADS, n_tok, HGRN_DK), F32),
                   jax.ShapeDtypeStruct((HGRN_HEADS, n_tok, HGRN_DK), F32),
                   jax.ShapeDtypeStruct((HGRN_HEADS, n_tok, HGRN_DK), BF16),
                   jax.ShapeDtypeStruct((n_tok, D_MODEL), BF16)],
        grid=(n_tok // tm,),
        in_specs=[pl.BlockSpec((tm, D_MODEL), lambda i: (i, 0)), _resident((1, D_MODEL)),
                  pl.BlockSpec((D_MODEL, 4 * D_MODEL), lambda i: (0, 0), pipeline_mode=once),
                  _resident((2, D_MODEL))],
        out_specs=[head_major, head_major, head_major,
                   pl.BlockSpec((tm, D_MODEL), lambda i: (i, 0))],
        compiler_params=_params("parallel"),
        name="hgrn_in_proj",
    )(x, gain, w_in, lb_params)


LEVELS = (32, 16, 8, 4, 2)


def _decay_sum_matrix():
    t = np.arange(CHUNK)[:, None]
    u = np.arange(CHUNK)[None, :]
    mats = [u <= t, u > t]
    for m in LEVELS:
        blk = t // m
        odd = (blk % 2) == 1
        q_side = (u >= blk * m) & (u <= t)
        k_side = (u > t) & (u < (blk + 1) * m)
        mats.append(np.where(odd, q_side, k_side))
    d = np.concatenate(mats, axis=0).astype(np.float32)
    return np.concatenate([d, d, d], axis=1)


def _level_index():
    t = np.arange(CHUNK)[:, None]
    s = np.arange(CHUNK)[None, :]
    x = t ^ s
    lvl = np.floor(np.log2(np.maximum(x, 1))).astype(np.int32)
    lvl = np.where(t == s, -1, lvl)
    return np.where(t < s, -2, lvl).astype(np.int32)


def _hgrn_scan_kernel(q_ref, f_ref, v_ref, gate_ref, gn_ref, dmat_ref, lvl_ref, a_ref, st_ref,
                      *, n_chunks):
    @pl.when(pl.program_id(1) == 0)
    def _():
        st_ref[...] = jnp.zeros_like(st_ref)

    pair_w = 2 * HGRN_DK
    row = lax.broadcasted_iota(jnp.int32, (CHUNK, pair_w), 0)
    lvl = lvl_ref[...]
    dmat = dmat_ref[...]
    zero_c = jnp.zeros((CHUNK, HGRN_DK), BF16)
    zero_s = jnp.zeros((HGRN_DK, HGRN_DK), BF16)

    def block_diag(a0, a1, zero):
        return jnp.concatenate([jnp.concatenate([a0, zero], axis=1),
                                jnp.concatenate([zero, a1], axis=1)], axis=0)

    pairs = range(HGRN_HEADS // 2)

    def gram(z):
        zb = block_diag(z[:, :HGRN_DK], z[:, HGRN_DK:], zero_c)
        return lax.dot_general(z, zb, NT_DIMS, preferred_element_type=F32)

    def load_and_sum(j):
        rows = slice(j * CHUNK, (j + 1) * CHUNK)
        out = []
        for p in pairs:
            cat = lambda ref: jnp.concatenate([ref[2 * p, rows, :], ref[2 * p + 1, rows, :]], axis=1)
            qt, fg, v = cat(q_ref), cat(f_ref), cat(v_ref)
            lf = jnp.log(fg)
            p1 = lf.astype(BF16)
            r1 = lf - p1.astype(F32)
            p2 = r1.astype(BF16)
            p3 = (r1 - p2.astype(F32)).astype(BF16)
            sums = jnp.dot(dmat, jnp.concatenate([p1, p2, p3], axis=0), preferred_element_type=F32)
            out.append((qt, fg, v, 1.0 - fg, sums))
        return out

    def scores_and_inter(staged):
        out = []
        for p in pairs:
            qt, fg, _, kk, sums = staged[p]
            qe = (qt * jnp.exp(sums[0:CHUNK])).astype(BF16)
            st_bd = block_diag(st_ref[2 * p].astype(BF16), st_ref[2 * p + 1].astype(BF16), zero_s)
            o_inter = lax.dot_general(qe, st_bd, NT_DIMS, preferred_element_type=F32)
            scores = jnp.zeros((CHUNK, 2 * CHUNK), F32)
            for li, m in enumerate(LEVELS):
                odd = ((row // m) % 2) == 1
                z = (jnp.where(odd, qt, kk) * jnp.exp(sums[(2 + li) * CHUNK:(3 + li) * CHUNK]))
                scores = jnp.where(lvl == int(np.log2(m)), gram(z.astype(BF16)), scores)
            z = jnp.where((row % 2) == 1, qt * fg, kk).astype(BF16)
            scores = jnp.where(lvl == 0, gram(z), scores)
            kkb = kk.astype(BF16)
            diag = lax.dot_general(qt.astype(BF16), block_diag(kkb[:, :HGRN_DK], kkb[:, HGRN_DK:], zero_c),
                                   NT_DIMS, preferred_element_type=F32)
            out.append((o_inter, jnp.where(lvl == -1, diag, scores)))
        return out

    def output_and_state(j, staged, scored):
        rows = slice(j * CHUNK, (j + 1) * CHUNK)
        outs = []
        for p in pairs:
            _, _, v, kk, sums = staged[p]
            o_inter, scores = scored[p]
            outs.append(o_inter + jnp.dot(scores.astype(BF16),
                                          block_diag(v[:, :HGRN_DK], v[:, HGRN_DK:], zero_c),
                                          preferred_element_type=F32))
            kd = (kk * jnp.exp(sums[CHUNK:2 * CHUNK])).astype(BF16)
            dec = jnp.exp(sums[CHUNK - 1:CHUNK, :])
            for hh, sl in ((2 * p, slice(0, HGRN_DK)), (2 * p + 1, slice(HGRN_DK, pair_w))):
                upd = lax.dot_general(v[:, sl], kd[:, sl], TN_DIMS, preferred_element_type=F32)
                st_ref[hh] = st_ref[hh] * dec[:, sl] + upd
        o_all = jnp.concatenate(outs, axis=1)
        a_ref[rows, :] = (_rmsnorm(o_all, gn_ref[...]) * gate_ref[rows, :].astype(F32)).astype(BF16)

    staged = load_and_sum(0)
    for j in range(n_chunks):
        scored = scores_and_inter(staged)
        nxt = load_and_sum(j + 1) if j + 1 < n_chunks else None
        output_and_state(j, staged, scored)
        staged = nxt


def _hgrn_scan(q, f, v, gate, g_norm, bsz, seq, tc=256):
    n_tok = q.shape[1]
    steps = seq // tc
    heads = pl.BlockSpec((HGRN_HEADS, tc, HGRN_DK), lambda b, c: (0, b * steps + c, 0))
    toks = pl.BlockSpec((tc, D_MODEL), lambda b, c: (b * steps + c, 0))
    dmat = jnp.asarray(_decay_sum_matrix(), BF16)
    lvl = jnp.asarray(np.tile(_level_index(), (1, 2)))
    return pl.pallas_call(
        functools.partial(_hgrn_scan_kernel, n_chunks=tc // CHUNK),
        out_shape=jax.ShapeDtypeStruct((n_tok, D_MODEL), BF16),
        grid=(bsz, steps),
        in_specs=[heads, heads, heads, toks, _resident((1, D_MODEL)), _resident(dmat.shape),
                  _resident(lvl.shape)],
        out_specs=toks,
        scratch_shapes=[pltpu.VMEM((HGRN_HEADS, HGRN_DK, HGRN_DK), F32)],
        compiler_params=_params("parallel", "arbitrary"),
        name="hgrn_scan",
    )(q, f, v, gate, g_norm, dmat, lvl)


def kernel(x, positions, mix_norm, mlp_norm, final_norm, attn_w_qkv, attn_b_qkv, attn_sinks,
           attn_w_o, hgrn_w_in, hgrn_g_norm, hgrn_w_o, hgrn_lower_bounds, mlp_w_up, mlp_w_down):
    bsz, seq, _ = x.shape
    n_tok = bsz * seq
    xf = x.reshape(n_tok, D_MODEL)
    gain = lambda g: g.reshape(1, D_MODEL).astype(F32)

    cos_t, sin_t = _rope_tables(positions)
    qt, k, vt = _qkv_proj(xf, cos_t, sin_t, gain(mix_norm[0]), attn_w_qkv[0].astype(BF16),
                          attn_b_qkv[0].reshape(1, QKV_DIM).astype(F32))
    a = _attention(qt, k, vt, attn_sinks[0].astype(F32), bsz, seq)
    xf = _out_mlp(a, xf, attn_w_o[0].astype(BF16), gain(mlp_norm[0]), mlp_w_up[0].astype(BF16),
                  mlp_w_down[0].astype(BF16), gain(final_norm), final_norm=False)

    hq, hf, hv, gate = _hgrn_in(xf, gain(mix_norm[1]), hgrn_w_in[0].astype(BF16),
                                hgrn_lower_bounds.astype(F32))
    a = _hgrn_scan(hq, hf, hv, gate, gain(hgrn_g_norm[0]), bsz, seq)
    xf = _out_mlp(a, xf, hgrn_w_o[0].astype(BF16), gain(mlp_norm[1]), mlp_w_up[1].astype(BF16),
                  mlp_w_down[1].astype(BF16), gain(final_norm), final_norm=True)
    return xf.reshape(bsz, seq, D_MODEL)
```

```python
import functools

import numpy as np
import jax
import jax.numpy as jnp
from jax import lax
from jax.experimental import pallas as pl
from jax.experimental.pallas import tpu as pltpu

D_MODEL = 1024
HEAD_DIM = 64
N_Q_HEADS = 16
N_KV_HEADS = 4
GROUP = 4
Q_DIM = N_Q_HEADS * HEAD_DIM
KV_DIM = N_KV_HEADS * HEAD_DIM
QKV_DIM = Q_DIM + 2 * KV_DIM
WINDOW = 128
BLOCK = 128
ROT_DIM = 16
ROT_HALF = ROT_DIM // 2
ROPE_THETA = 500000.0
NEG_INF = -1e30
HGRN_HEADS = 8
HGRN_DK = 128
CHUNK = 64
D_FF = 4 * D_MODEL
FF_CHUNK = 1024
NORM_EPS = 1e-5
LOG2E = 1.4426950408889634

LANES = 128
VMEM_LIMIT = 56 * 1024 * 1024

BF16 = jnp.bfloat16
F32 = jnp.float32

NT_DIMS = (((1,), (1,)), ((), ()))
TN_DIMS = (((0,), (0,)), ((), ()))


def _rmsnorm(x, gain):
    ms = jnp.mean(x * x, axis=-1, keepdims=True)
    return x * lax.rsqrt(ms + NORM_EPS) * gain


def _params(*sem):
    return pltpu.CompilerParams(dimension_semantics=sem, vmem_limit_bytes=VMEM_LIMIT)


def _resident(shape):
    return pl.BlockSpec(shape, lambda *_: (0,) * len(shape))


def _rope_kernel(pos_ref, invf_ref, cos_ref, sin_ref):
    ang = pos_ref[...].astype(F32) * invf_ref[...]
    cos_ref[...] = jnp.cos(ang)
    sin_ref[...] = jnp.sin(ang)


def _rope_tables(positions, tb=2048):
    n_tok = positions.size
    inv_freq = ROPE_THETA ** (-jnp.arange(0, ROT_DIM, 2, dtype=F32) / ROT_DIM)
    table = pl.BlockSpec((ROT_HALF, tb), lambda i: (0, i))
    return pl.pallas_call(
        _rope_kernel,
        out_shape=[jax.ShapeDtypeStruct((ROT_HALF, n_tok), F32)] * 2,
        grid=(n_tok // tb,),
        in_specs=[pl.BlockSpec((1, tb), lambda i: (0, i)), _resident((ROT_HALF, 1))],
        out_specs=[table, table],
        compiler_params=_params("parallel"),
        name="rope_tables",
    )(positions.reshape(1, n_tok), inv_freq.reshape(ROT_HALF, 1))


def _qkv_kernel(x_ref, cos_ref, sin_ref, g_ref, w_ref, b_ref, qt_ref, k_ref, vt_ref):
    h = _rmsnorm(x_ref[...], g_ref[...]).astype(BF16)
    qkv = jnp.dot(h, w_ref[...], preferred_element_type=F32) + b_ref[...]
    cos_t = cos_ref[...]
    sin_t = sin_ref[...]

    def rotate_t(tt):
        parts = []
        for hd in range(LANES // HEAD_DIM):
            base = hd * HEAD_DIM
            x1 = tt[base:base + ROT_HALF]
            x2 = tt[base + ROT_HALF:base + ROT_DIM]
            parts += [x1 * cos_t - x2 * sin_t, x2 * cos_t + x1 * sin_t,
                      tt[base + ROT_DIM:base + HEAD_DIM]]
        return jnp.concatenate(parts, axis=0)

    scale = HEAD_DIM ** -0.5 * LOG2E
    for j in range(Q_DIM // LANES):
        t = qkv[:, j * LANES:(j + 1) * LANES] * scale
        qt_ref[j * LANES:(j + 1) * LANES, :] = rotate_t(t.T).astype(BF16)
    for j in range(KV_DIM // LANES):
        t = qkv[:, Q_DIM + j * LANES:Q_DIM + (j + 1) * LANES]
        k_ref[:, j * LANES:(j + 1) * LANES] = rotate_t(t.T).T.astype(BF16)
        t = qkv[:, Q_DIM + KV_DIM + j * LANES:Q_DIM + KV_DIM + (j + 1) * LANES]
        vt_ref[j * LANES:(j + 1) * LANES, :] = t.T.astype(BF16)


def _qkv_proj(x, cos_t, sin_t, gain, w_qkv, b_qkv, tm=1024):
    n_tok = x.shape[0]
    row = lambda w: pl.BlockSpec((tm, w), lambda i: (i, 0))
    col = lambda h: pl.BlockSpec((h, tm), lambda i: (0, i))
    return pl.pallas_call(
        _qkv_kernel,
        out_shape=[jax.ShapeDtypeStruct((Q_DIM, n_tok), BF16),
                   jax.ShapeDtypeStruct((n_tok, KV_DIM), BF16),
                   jax.ShapeDtypeStruct((KV_DIM, n_tok), BF16)],
        grid=(n_tok // tm,),
        in_specs=[row(D_MODEL), col(ROT_HALF), col(ROT_HALF), _resident((1, D_MODEL)),
                  _resident((D_MODEL, QKV_DIM)), _resident((1, QKV_DIM))],
        out_specs=[col(Q_DIM), row(KV_DIM), col(KV_DIM)],
        compiler_params=_params("parallel"),
        name="qkv_rope",
    )(x, cos_t, sin_t, gain, w_qkv, b_qkv)


def _attn_kernel(sink_ref, qt_ref, kp_ref, kc_ref, vtp_ref, vtc_ref, o_ref, *, q_blocks):
    n = pl.program_id(1)
    band = 2 * BLOCK
    width = GROUP * BLOCK
    kj = lax.broadcasted_iota(jnp.int32, (band, width), 0)
    qi = lax.broadcasted_iota(jnp.int32, (band, width), 1) % BLOCK
    delta = qi + BLOCK - kj
    in_window = (delta >= 0) & (delta < WINDOW)
    valid_first = in_window & ((kj >= BLOCK) | (n > 0))
    head_of_lane = lax.broadcasted_iota(jnp.int32, (1, width), 1) // BLOCK
    zero_q = jnp.zeros((HEAD_DIM, width), BF16)
    ones_rows = jnp.ones((16, band), BF16)

    def scores(j, kvh):
        tile = slice((kvh // 2) * LANES, (kvh // 2 + 1) * LANES)
        if j == 0:
            kband = jnp.concatenate([kp_ref[:, tile], kc_ref[:BLOCK, tile]], axis=0)
        else:
            kband = kc_ref[(j - 1) * BLOCK:(j + 1) * BLOCK, tile]
        qc = slice(j * BLOCK, (j + 1) * BLOCK)
        qt4 = jnp.concatenate(
            [qt_ref[(GROUP * kvh + g) * HEAD_DIM:(GROUP * kvh + g + 1) * HEAD_DIM, qc]
             for g in range(GROUP)], axis=1)
        rhs = jnp.concatenate([qt4, zero_q] if kvh % 2 == 0 else [zero_q, qt4], axis=0)
        s = jnp.dot(kband, rhs, preferred_element_type=F32)
        return jnp.where(valid_first if j == 0 else in_window, s, NEG_INF)

    def softmax_pv(j, kvh, s):
        hs = slice(kvh * HEAD_DIM, (kvh + 1) * HEAD_DIM)
        if j == 0:
            vt = jnp.concatenate([vtp_ref[hs, :], vtc_ref[hs, :BLOCK]], axis=1)
        else:
            vt = vtc_ref[hs, (j - 1) * BLOCK:(j + 1) * BLOCK]
        sink = jnp.zeros((1, width), F32)
        for g in range(GROUP):
            sink = jnp.where(head_of_lane == g, sink_ref[GROUP * kvh + g] * LOG2E, sink)
        m = jnp.maximum(jnp.max(s, axis=0, keepdims=True), sink)
        e = jnp.exp2(s - m).astype(BF16)
        pv = jnp.dot(jnp.concatenate([vt, ones_rows], axis=0), e,
                     preferred_element_type=F32)
        den = pv[HEAD_DIM:HEAD_DIM + 1, :] + jnp.exp2(sink - m)
        out_t = pv[:HEAD_DIM, :] * (1.0 / den)
        qr = slice(j * BLOCK, (j + 1) * BLOCK)
        for t in range(2):
            pair = jnp.concatenate([out_t[:, (2 * t) * BLOCK:(2 * t + 1) * BLOCK],
                                    out_t[:, (2 * t + 1) * BLOCK:(2 * t + 2) * BLOCK]], axis=0)
            o_ref[qr, (2 * kvh + t) * LANES:(2 * kvh + t + 1) * LANES] = pair.T.astype(BF16)

    items = [(j, kvh) for j in range(q_blocks) for kvh in range(N_KV_HEADS)]
    s = scores(*items[0])
    for i, item in enumerate(items):
        s_next = scores(*items[i + 1]) if i + 1 < len(items) else None
        softmax_pv(*item, s)
        s = s_next


def _attention(qt, k, vt, sinks, bsz, seq, q_blocks=2):
    tq = q_blocks * BLOCK
    steps = seq // tq
    nb = seq // BLOCK
    cur = lambda b, n: (b * steps + n, 0)
    prev = lambda b, n: (b * nb + jnp.maximum(n * q_blocks - 1, 0), 0)
    cur_t = lambda b, n: (0, b * steps + n)
    prev_t = lambda b, n: (0, b * nb + jnp.maximum(n * q_blocks - 1, 0))
    return pl.pallas_call(
        functools.partial(_attn_kernel, q_blocks=q_blocks),
        out_shape=jax.ShapeDtypeStruct((bsz * seq, Q_DIM), BF16),
        grid=(bsz, steps),
        in_specs=[pl.BlockSpec(memory_space=pltpu.SMEM),
                  pl.BlockSpec((Q_DIM, tq), cur_t),
                  pl.BlockSpec((BLOCK, KV_DIM), prev), pl.BlockSpec((tq, KV_DIM), cur),
                  pl.BlockSpec((KV_DIM, BLOCK), prev_t), pl.BlockSpec((KV_DIM, tq), cur_t)],
        out_specs=pl.BlockSpec((tq, Q_DIM), cur),
        compiler_params=_params("parallel", "parallel"),
        name="swa_attention",
    )(sinks, qt, k, k, vt, vt)


def _out_mlp_kernel(a_ref, x_ref, wo_ref, g_ref, wu_ref, wd_ref, gn_ref, *out_refs, last):
    x1 = x_ref[...] + jnp.dot(a_ref[...], wo_ref[...], preferred_element_type=F32)
    h = _rmsnorm(x1, g_ref[...]).astype(BF16)
    x2 = x1
    for c in range(D_FF // FF_CHUNK):
        cols = slice(c * FF_CHUNK, (c + 1) * FF_CHUNK)
        u = jnp.maximum(jnp.dot(h, wu_ref[:, cols], preferred_element_type=F32), 0.0)
        x2 = x2 + jnp.dot((u * u).astype(BF16), wd_ref[cols, :], preferred_element_type=F32)
    normed = _rmsnorm(x2, gn_ref[...])
    if last:
        out_refs[0][...] = normed
    else:
        out_refs[0][...] = x2
        out_refs[1][...] = normed.astype(BF16)


def _out_mlp(a, x, w_o, gain, w_up, w_down, next_gain, last, tm=1024):
    n_tok = x.shape[0]
    row = pl.BlockSpec((tm, D_MODEL), lambda i: (i, 0))
    once = pl.Buffered(1)
    wspec = lambda shape: pl.BlockSpec(shape, lambda i: (0, 0), pipeline_mode=once)
    stream = jax.ShapeDtypeStruct((n_tok, D_MODEL), F32)
    return pl.pallas_call(
        functools.partial(_out_mlp_kernel, last=last),
        out_shape=stream if last else [stream, jax.ShapeDtypeStruct((n_tok, D_MODEL), BF16)],
        grid=(n_tok // tm,),
        in_specs=[row, row, wspec((D_MODEL, D_MODEL)), wspec((1, D_MODEL)),
                  wspec((D_MODEL, D_FF)), wspec((D_FF, D_MODEL)), wspec((1, D_MODEL))],
        out_specs=row if last else [row, row],
        compiler_params=_params("parallel"),
        name="out_proj_mlp",
    )(a, x, w_o, gain, w_up, w_down, next_gain)


def _hgrn_in_kernel(h_ref, w_ref, lbp_ref, q_ref, f_ref, v_ref, gate_ref):
    h = h_ref[...]
    lbp = lbp_ref[...]
    e = jnp.exp(lbp - jnp.max(lbp, axis=0, keepdims=True))
    sm = e / jnp.sum(e, axis=0, keepdims=True)
    lb = (sm[0:1] + sm[1:2]) - sm[0:1]

    def proj(c):
        return jnp.dot(h, w_ref[:, c * D_MODEL:(c + 1) * D_MODEL], preferred_element_type=F32)

    def by_head(ref, val):
        for hd in range(HGRN_HEADS):
            ref[hd] = val[:, hd * HGRN_DK:(hd + 1) * HGRN_DK]

    q = proj(0)
    f = proj(1)
    by_head(q_ref, q * jax.nn.sigmoid(q))
    i = proj(2)
    by_head(f_ref, lb + (1.0 - lb) * jax.nn.sigmoid(f))
    g = proj(3)
    by_head(v_ref, i.astype(BF16))
    gate_ref[...] = (g * jax.nn.sigmoid(g)).astype(BF16)


def _hgrn_in(h, w_in, lb_params, tm=1024):
    n_tok = h.shape[0]
    head_major = pl.BlockSpec((HGRN_HEADS, tm, HGRN_DK), lambda i: (0, i, 0))
    once = pl.Buffered(1)
    return pl.pallas_call(
        _hgrn_in_kernel,
        out_shape=[jax.ShapeDtypeStruct((HGRN_HEADS, n_tok, HGRN_DK), F32),
                   jax.ShapeDtypeStruct((HGRN_HEADS, n_tok, HGRN_DK), F32),
                   jax.ShapeDtypeStruct((HGRN_HEADS, n_tok, HGRN_DK), BF16),
                   jax.ShapeDtypeStruct((n_tok, D_MODEL), BF16)],
        grid=(n_tok // tm,),
        in_specs=[pl.BlockSpec((tm, D_MODEL), lambda i: (i, 0)),
                  pl.BlockSpec((D_MODEL, 4 * D_MODEL), lambda i: (0, 0), pipeline_mode=once),
                  _resident((2, D_MODEL))],
        out_specs=[head_major, head_major, head_major,
                   pl.BlockSpec((tm, D_MODEL), lambda i: (i, 0))],
        compiler_params=_params("parallel"),
        name="hgrn_in_proj",
    )(h, w_in, lb_params)


LEVELS = (32, 16, 8, 4, 2)
SPLIT = 2


def _decay_sum_matrix():
    t = np.arange(CHUNK)[:, None]
    u = np.arange(CHUNK)[None, :]
    mats = [u <= t]
    for m in LEVELS:
        blk = t // m
        odd = (blk % 2) == 1
        q_side = (u >= blk * m) & (u <= t)
        k_side = (u > t) & (u < (blk + 1) * m)
        mats.append(np.where(odd, q_side, k_side))
    d = np.concatenate(mats, axis=0).astype(np.float32)
    return np.concatenate([d] * SPLIT, axis=1)


def _level_index():
    t = np.arange(CHUNK)[:, None]
    s = np.arange(CHUNK)[None, :]
    x = t ^ s
    lvl = np.floor(np.log2(np.maximum(x, 1))).astype(np.int32)
    lvl = np.where(t == s, -1, lvl)
    return np.where(t < s, -2, lvl).astype(np.int32)


def _hgrn_scan_kernel(q_ref, f_ref, v_ref, gate_ref, gn_ref, dmat_ref, lvl_ref, a_ref, st_ref,
                      *, n_chunks):
    @pl.when(pl.program_id(1) == 0)
    def _():
        st_ref[...] = jnp.zeros_like(st_ref)

    pair_w = 2 * HGRN_DK
    row = lax.broadcasted_iota(jnp.int32, (CHUNK, pair_w), 0)
    lvl = lvl_ref[...]
    dmat = dmat_ref[...]
    zero_c = jnp.zeros((CHUNK, HGRN_DK), BF16)
    zero_s = jnp.zeros((HGRN_DK, HGRN_DK), BF16)

    def block_diag(a0, a1, zero):
        return jnp.concatenate([jnp.concatenate([a0, zero], axis=1),
                                jnp.concatenate([zero, a1], axis=1)], axis=0)

    pairs = range(HGRN_HEADS // 2)

    def gram(z):
        zb = block_diag(z[:, :HGRN_DK], z[:, HGRN_DK:], zero_c)
        return lax.dot_general(z, zb, NT_DIMS, preferred_element_type=F32)

    def load_and_sum(j):
        rows = slice(j * CHUNK, (j + 1) * CHUNK)
        out = []
        for p in pairs:
            cat = lambda ref: jnp.concatenate([ref[2 * p, rows, :], ref[2 * p + 1, rows, :]], axis=1)
            qt, fg, v = cat(q_ref), cat(f_ref), cat(v_ref)
            lf = jnp.log2(fg)
            pieces, rest = [], lf
            for _ in range(SPLIT):
                piece = rest.astype(BF16)
                pieces.append(piece)
                rest = rest - piece.astype(F32)
            sums = jnp.dot(dmat, jnp.concatenate(pieces, axis=0), preferred_element_type=F32)
            out.append((qt, fg, v, 1.0 - fg, sums))
        return out

    def scores_and_inter(staged):
        out = []
        for p in pairs:
            qt, fg, _, kk, sums = staged[p]
            qe = (qt * jnp.exp2(sums[0:CHUNK])).astype(BF16)
            st_bd = block_diag(st_ref[2 * p].astype(BF16), st_ref[2 * p + 1].astype(BF16), zero_s)
            o_inter = jnp.dot(qe, st_bd, preferred_element_type=F32)
            scores = jnp.zeros((CHUNK, 2 * CHUNK), F32)
            for li, m in enumerate(LEVELS):
                if m >= 8:
                    side = jnp.concatenate([(qt if i % 2 else kk)[i * m:(i + 1) * m]
                                            for i in range(CHUNK // m)], axis=0)
                else:
                    side = jnp.where(((row // m) % 2) == 1, qt, kk)
                z = side * jnp.exp2(sums[(1 + li) * CHUNK:(2 + li) * CHUNK])
                scores = jnp.where(lvl == int(np.log2(m)), gram(z.astype(BF16)), scores)
            z = jnp.where((row % 2) == 1, qt * fg, kk).astype(BF16)
            scores = jnp.where(lvl == 0, gram(z), scores)
            kkb = kk.astype(BF16)
            diag = lax.dot_general(qt.astype(BF16), block_diag(kkb[:, :HGRN_DK], kkb[:, HGRN_DK:], zero_c),
                                   NT_DIMS, preferred_element_type=F32)
            out.append((o_inter, jnp.where(lvl == -1, diag, scores)))
        return out

    def output_and_state(j, staged, scored):
        rows = slice(j * CHUNK, (j + 1) * CHUNK)
        outs = []
        for p in pairs:
            _, _, v, kk, sums = staged[p]
            o_inter, scores = scored[p]
            outs.append(o_inter + jnp.dot(scores.astype(BF16),
                                          block_diag(v[:, :HGRN_DK], v[:, HGRN_DK:], zero_c),
                                          preferred_element_type=F32))
            b = sums[0:CHUNK]
            b_last = b[CHUNK - 1:CHUNK, :]
            kd = (kk * jnp.exp2(b_last - b)).astype(BF16)
            dec = jnp.exp2(b_last)
            for hh, sl in ((2 * p, slice(0, HGRN_DK)), (2 * p + 1, slice(HGRN_DK, pair_w))):
                upd = lax.dot_general(kd[:, sl], v[:, sl], TN_DIMS, preferred_element_type=F32)
                dec_rows = jnp.broadcast_to(dec[:, sl], (HGRN_DK, HGRN_DK)).T
                st_ref[hh] = st_ref[hh] * dec_rows + upd
        o_all = jnp.concatenate(outs, axis=1)
        a_ref[rows, :] = (_rmsnorm(o_all, gn_ref[...]) * gate_ref[rows, :].astype(F32)).astype(BF16)

    staged = load_and_sum(0)
    for j in range(n_chunks):
        scored = scores_and_inter(staged)
        nxt = load_and_sum(j + 1) if j + 1 < n_chunks else None
        output_and_state(j, staged, scored)
        staged = nxt


def _hgrn_scan(q, f, v, gate, g_norm, bsz, seq, tc=256):
    n_tok = q.shape[1]
    steps = seq // tc
    heads = pl.BlockSpec((HGRN_HEADS, tc, HGRN_DK), lambda b, c: (0, b * steps + c, 0))
    toks = pl.BlockSpec((tc, D_MODEL), lambda b, c: (b * steps + c, 0))
    dmat = jnp.asarray(_decay_sum_matrix(), BF16)
    lvl = jnp.asarray(np.tile(_level_index(), (1, 2)))
    return pl.pallas_call(
        functools.partial(_hgrn_scan_kernel, n_chunks=tc // CHUNK),
        out_shape=jax.ShapeDtypeStruct((n_tok, D_MODEL), BF16),
        grid=(bsz, steps),
        in_specs=[heads, heads, heads, toks, _resident((1, D_MODEL)), _resident(dmat.shape),
                  _resident(lvl.shape)],
        out_specs=toks,
        scratch_shapes=[pltpu.VMEM((HGRN_HEADS, HGRN_DK, HGRN_DK), F32)],
        compiler_params=_params("parallel", "arbitrary"),
        name="hgrn_scan",
    )(q, f, v, gate, g_norm, dmat, lvl)


def kernel(x, positions, mix_norm, mlp_norm, final_norm, attn_w_qkv, attn_b_qkv, attn_sinks,
           attn_w_o, hgrn_w_in, hgrn_g_norm, hgrn_w_o, hgrn_lower_bounds, mlp_w_up, mlp_w_down):
    bsz, seq, _ = x.shape
    n_tok = bsz * seq
    xf = x.reshape(n_tok, D_MODEL)
    gain = lambda g: g.reshape(1, D_MODEL).astype(F32)

    cos_t, sin_t = _rope_tables(positions)
    qt, k, vt = _qkv_proj(xf, cos_t, sin_t, gain(mix_norm[0]), attn_w_qkv[0].astype(BF16),
                          attn_b_qkv[0].reshape(1, QKV_DIM).astype(F32))
    a = _attention(qt, k, vt, attn_sinks[0].astype(F32), bsz, seq)
    xf, h1 = _out_mlp(a, xf, attn_w_o[0].astype(BF16), gain(mlp_norm[0]), mlp_w_up[0].astype(BF16),
                      mlp_w_down[0].astype(BF16), gain(mix_norm[1]), last=False)

    hq, hf, hv, gate = _hgrn_in(h1, hgrn_w_in[0].astype(BF16), hgrn_lower_bounds.astype(F32))
    a = _hgrn_scan(hq, hf, hv, gate, gain(hgrn_g_norm[0]), bsz, seq)
    out = _out_mlp(a, xf, hgrn_w_o[0].astype(BF16), gain(mlp_norm[1]), mlp_w_up[1].astype(BF16),
                   mlp_w_down[1].astype(BF16), gain(final_norm), last=True)
    return out.reshape(bsz, seq, D_MODEL)
```

```python
import functools

import numpy as np
import jax
import jax.numpy as jnp
from jax import lax
from jax.experimental import pallas as pl
from jax.experimental.pallas import tpu as pltpu

D_MODEL = 1024
HEAD_DIM = 64
N_Q_HEADS = 16
N_KV_HEADS = 4
GROUP = 4
Q_DIM = N_Q_HEADS * HEAD_DIM
KV_DIM = N_KV_HEADS * HEAD_DIM
QKV_DIM = Q_DIM + 2 * KV_DIM
WINDOW = 128
BLOCK = 128
ROT_DIM = 16
ROT_HALF = ROT_DIM // 2
ROPE_THETA = 500000.0
NEG_INF = -1e30
HGRN_HEADS = 8
HGRN_DK = 128
CHUNK = 64
D_FF = 4 * D_MODEL
FF_CHUNK = 1024
AHEAD = 2
NORM_EPS = 1e-5
LOG2E = 1.4426950408889634

LANES = 128
VMEM_LIMIT = 56 * 1024 * 1024

BF16 = jnp.bfloat16
F32 = jnp.float32

NT_DIMS = (((1,), (1,)), ((), ()))
TN_DIMS = (((0,), (0,)), ((), ()))


def _rmsnorm(x, gain):
    ms = jnp.mean(x * x, axis=-1, keepdims=True)
    return x * lax.rsqrt(ms + NORM_EPS) * gain


def _params(*sem):
    return pltpu.CompilerParams(dimension_semantics=sem, vmem_limit_bytes=VMEM_LIMIT)


def _resident(shape):
    return pl.BlockSpec(shape, lambda *_: (0,) * len(shape))


def _rope_kernel(pos_ref, invf_ref, cos_ref, sin_ref):
    ang = pos_ref[...].astype(F32) * invf_ref[...]
    cos_ref[...] = jnp.cos(ang)
    sin_ref[...] = jnp.sin(ang)


def _rope_tables(positions, tb=2048):
    n_tok = positions.size
    inv_freq = ROPE_THETA ** (-jnp.arange(0, ROT_DIM, 2, dtype=F32) / ROT_DIM)
    table = pl.BlockSpec((ROT_HALF, tb), lambda i: (0, i))
    return pl.pallas_call(
        _rope_kernel,
        out_shape=[jax.ShapeDtypeStruct((ROT_HALF, n_tok), F32)] * 2,
        grid=(n_tok // tb,),
        in_specs=[pl.BlockSpec((1, tb), lambda i: (0, i)), _resident((ROT_HALF, 1))],
        out_specs=[table, table],
        compiler_params=_params("parallel"),
        name="rope_tables",
    )(positions.reshape(1, n_tok), inv_freq.reshape(ROT_HALF, 1))


def _qkv_kernel(x_ref, cos_ref, sin_ref, g_ref, w_ref, b_ref, qt_ref, k_ref, vt_ref):
    h = _rmsnorm(x_ref[...], g_ref[...]).astype(BF16)
    qkv = jnp.dot(h, w_ref[...], preferred_element_type=F32) + b_ref[...]
    cos_t = cos_ref[...]
    sin_t = sin_ref[...]

    def rotate_t(tt):
        parts = []
        for hd in range(LANES // HEAD_DIM):
            base = hd * HEAD_DIM
            x1 = tt[base:base + ROT_HALF]
            x2 = tt[base + ROT_HALF:base + ROT_DIM]
            parts += [x1 * cos_t - x2 * sin_t, x2 * cos_t + x1 * sin_t,
                      tt[base + ROT_DIM:base + HEAD_DIM]]
        return jnp.concatenate(parts, axis=0)

    scale = HEAD_DIM ** -0.5 * LOG2E
    for j in range(Q_DIM // LANES):
        t = qkv[:, j * LANES:(j + 1) * LANES] * scale
        qt_ref[j * LANES:(j + 1) * LANES, :] = rotate_t(t.T).astype(BF16)
    for j in range(KV_DIM // LANES):
        t = qkv[:, Q_DIM + j * LANES:Q_DIM + (j + 1) * LANES]
        k_ref[:, j * LANES:(j + 1) * LANES] = rotate_t(t.T).T.astype(BF16)
        t = qkv[:, Q_DIM + KV_DIM + j * LANES:Q_DIM + KV_DIM + (j + 1) * LANES]
        vt_ref[j * LANES:(j + 1) * LANES, :] = t.T.astype(BF16)


def _qkv_proj(x, cos_t, sin_t, gain, w_qkv, b_qkv, tm=1024):
    n_tok = x.shape[0]
    row = lambda w: pl.BlockSpec((tm, w), lambda i: (i, 0))
    col = lambda h: pl.BlockSpec((h, tm), lambda i: (0, i))
    return pl.pallas_call(
        _qkv_kernel,
        out_shape=[jax.ShapeDtypeStruct((Q_DIM, n_tok), BF16),
                   jax.ShapeDtypeStruct((n_tok, KV_DIM), BF16),
                   jax.ShapeDtypeStruct((KV_DIM, n_tok), BF16)],
        grid=(n_tok // tm,),
        in_specs=[row(D_MODEL), col(ROT_HALF), col(ROT_HALF), _resident((1, D_MODEL)),
                  _resident((D_MODEL, QKV_DIM)), _resident((1, QKV_DIM))],
        out_specs=[col(Q_DIM), row(KV_DIM), col(KV_DIM)],
        compiler_params=_params("parallel"),
        name="qkv_rope",
    )(x, cos_t, sin_t, gain, w_qkv, b_qkv)


def _attn_kernel(sink_ref, qt_ref, kp_ref, kc_ref, vtp_ref, vtc_ref, o_ref, bias_ref, *, q_blocks):
    n = pl.program_id(1)
    band = 2 * BLOCK
    width = GROUP * BLOCK
    kj = lax.broadcasted_iota(jnp.int32, (band, width), 0)
    qi = lax.broadcasted_iota(jnp.int32, (band, width), 1) % BLOCK
    delta = qi + BLOCK - kj
    in_window = (delta >= 0) & (delta < WINDOW)
    bias_ref[0] = jnp.where(in_window & ((kj >= BLOCK) | (n > 0)), 0.0, NEG_INF)
    bias_ref[1] = jnp.where(in_window, 0.0, NEG_INF)
    head_of_lane = lax.broadcasted_iota(jnp.int32, (1, width), 1) // BLOCK
    zero_q = jnp.zeros((HEAD_DIM, width), BF16)
    ones_rows = jnp.ones((16, band), BF16)

    def scores(j, kvh):
        tile = slice((kvh // 2) * LANES, (kvh // 2 + 1) * LANES)
        if j == 0:
            kband = jnp.concatenate([kp_ref[:, tile], kc_ref[:BLOCK, tile]], axis=0)
        else:
            kband = kc_ref[(j - 1) * BLOCK:(j + 1) * BLOCK, tile]
        qc = slice(j * BLOCK, (j + 1) * BLOCK)
        qt4 = jnp.concatenate(
            [qt_ref[(GROUP * kvh + g) * HEAD_DIM:(GROUP * kvh + g + 1) * HEAD_DIM, qc]
             for g in range(GROUP)], axis=1)
        rhs = jnp.concatenate([qt4, zero_q] if kvh % 2 == 0 else [zero_q, qt4], axis=0)
        s = jnp.dot(kband, rhs, preferred_element_type=F32)
        return s + bias_ref[0 if j == 0 else 1]

    def softmax_pv(j, kvh, s):
        hs = slice(kvh * HEAD_DIM, (kvh + 1) * HEAD_DIM)
        if j == 0:
            vt = jnp.concatenate([vtp_ref[hs, :], vtc_ref[hs, :BLOCK]], axis=1)
        else:
            vt = vtc_ref[hs, (j - 1) * BLOCK:(j + 1) * BLOCK]
        sink = jnp.zeros((1, width), F32)
        for g in range(GROUP):
            sink = jnp.where(head_of_lane == g, sink_ref[GROUP * kvh + g] * LOG2E, sink)
        m = jnp.maximum(jnp.max(s, axis=0, keepdims=True), sink)
        e = jnp.exp2(s - m).astype(BF16)
        pv = jnp.dot(jnp.concatenate([vt, ones_rows], axis=0), e,
                     preferred_element_type=F32)
        den = pv[HEAD_DIM:HEAD_DIM + 1, :] + jnp.exp2(sink - m)
        out_t = pv[:HEAD_DIM, :] * (1.0 / den)
        qr = slice(j * BLOCK, (j + 1) * BLOCK)
        for t in range(2):
            pair = jnp.concatenate([out_t[:, (2 * t) * BLOCK:(2 * t + 1) * BLOCK],
                                    out_t[:, (2 * t + 1) * BLOCK:(2 * t + 2) * BLOCK]], axis=0)
            o_ref[qr, (2 * kvh + t) * LANES:(2 * kvh + t + 1) * LANES] = pair.T.astype(BF16)

    items = [(j, kvh) for j in range(q_blocks) for kvh in range(N_KV_HEADS)]
    pending = [scores(*item) for item in items[:AHEAD]]
    for i, item in enumerate(items):
        if i + AHEAD < len(items):
            pending.append(scores(*items[i + AHEAD]))
        softmax_pv(*item, pending.pop(0))


def _attention(qt, k, vt, sinks, bsz, seq, q_blocks=8):
    tq = q_blocks * BLOCK
    steps = seq // tq
    nb = seq // BLOCK
    cur = lambda b, n: (b * steps + n, 0)
    prev = lambda b, n: (b * nb + jnp.maximum(n * q_blocks - 1, 0), 0)
    cur_t = lambda b, n: (0, b * steps + n)
    prev_t = lambda b, n: (0, b * nb + jnp.maximum(n * q_blocks - 1, 0))
    return pl.pallas_call(
        functools.partial(_attn_kernel, q_blocks=q_blocks),
        out_shape=jax.ShapeDtypeStruct((bsz * seq, Q_DIM), BF16),
        grid=(bsz, steps),
        in_specs=[pl.BlockSpec(memory_space=pltpu.SMEM),
                  pl.BlockSpec((Q_DIM, tq), cur_t),
                  pl.BlockSpec((BLOCK, KV_DIM), prev), pl.BlockSpec((tq, KV_DIM), cur),
                  pl.BlockSpec((KV_DIM, BLOCK), prev_t), pl.BlockSpec((KV_DIM, tq), cur_t)],
        out_specs=pl.BlockSpec((tq, Q_DIM), cur),
        scratch_shapes=[pltpu.VMEM((2, 2 * BLOCK, GROUP * BLOCK), F32)],
        compiler_params=_params("parallel", "parallel"),
        name="swa_attention",
    )(sinks, qt, k, k, vt, vt)


def _out_mlp_kernel(a_ref, x_ref, wo_ref, g_ref, wu_ref, wd_ref, gn_ref, *out_refs, last):
    x1 = x_ref[...] + jnp.dot(a_ref[...], wo_ref[...], preferred_element_type=F32)
    h = _rmsnorm(x1, g_ref[...]).astype(BF16)
    x2 = x1
    for c in range(D_FF // FF_CHUNK):
        cols = slice(c * FF_CHUNK, (c + 1) * FF_CHUNK)
        u = jnp.maximum(jnp.dot(h, wu_ref[:, cols], preferred_element_type=F32), 0.0)
        x2 = x2 + jnp.dot((u * u).astype(BF16), wd_ref[cols, :], preferred_element_type=F32)
    normed = _rmsnorm(x2, gn_ref[...])
    if last:
        out_refs[0][...] = normed
    else:
        out_refs[0][...] = x2
        out_refs[1][...] = normed.astype(BF16)


def _out_mlp(a, x, w_o, gain, w_up, w_down, next_gain, last, tm=1024):
    n_tok = x.shape[0]
    row = pl.BlockSpec((tm, D_MODEL), lambda i: (i, 0))
    once = pl.Buffered(1)
    wspec = lambda shape: pl.BlockSpec(shape, lambda i: (0, 0), pipeline_mode=once)
    stream = jax.ShapeDtypeStruct((n_tok, D_MODEL), F32)
    return pl.pallas_call(
        functools.partial(_out_mlp_kernel, last=last),
        out_shape=stream if last else [stream, jax.ShapeDtypeStruct((n_tok, D_MODEL), BF16)],
        grid=(n_tok // tm,),
        in_specs=[row, row, wspec((D_MODEL, D_MODEL)), wspec((1, D_MODEL)),
                  wspec((D_MODEL, D_FF)), wspec((D_FF, D_MODEL)), wspec((1, D_MODEL))],
        out_specs=row if last else [row, row],
        compiler_params=_params("parallel"),
        name="out_proj_mlp",
    )(a, x, w_o, gain, w_up, w_down, next_gain)


def _hgrn_in_kernel(h_ref, w_ref, lbp_ref, q_ref, f_ref, v_ref, gate_ref):
    h = h_ref[...]
    lbp = lbp_ref[...]
    e = jnp.exp(lbp - jnp.max(lbp, axis=0, keepdims=True))
    sm = e / jnp.sum(e, axis=0, keepdims=True)
    lb = (sm[0:1] + sm[1:2]) - sm[0:1]

    def proj(c):
        return jnp.dot(h, w_ref[:, c * D_MODEL:(c + 1) * D_MODEL], preferred_element_type=F32)

    def by_head(ref, val):
        for hd in range(HGRN_HEADS):
            ref[hd] = val[:, hd * HGRN_DK:(hd + 1) * HGRN_DK]

    q = proj(0)
    f = proj(1)
    by_head(q_ref, (q * jax.nn.sigmoid(q)).astype(BF16))
    i = proj(2)
    by_head(f_ref, lb + (1.0 - lb) * jax.nn.sigmoid(f))
    g = proj(3)
    by_head(v_ref, i.astype(BF16))
    gate_ref[...] = (g * jax.nn.sigmoid(g)).astype(BF16)


def _hgrn_in(h, w_in, lb_params, tm=1024):
    n_tok = h.shape[0]
    head_major = pl.BlockSpec((HGRN_HEADS, tm, HGRN_DK), lambda i: (0, i, 0))
    once = pl.Buffered(1)
    return pl.pallas_call(
        _hgrn_in_kernel,
        out_shape=[jax.ShapeDtypeStruct((HGRN_HEADS, n_tok, HGRN_DK), BF16),
                   jax.ShapeDtypeStruct((HGRN_HEADS, n_tok, HGRN_DK), F32),
                   jax.ShapeDtypeStruct((HGRN_HEADS, n_tok, HGRN_DK), BF16),
                   jax.ShapeDtypeStruct((n_tok, D_MODEL), BF16)],
        grid=(n_tok // tm,),
        in_specs=[pl.BlockSpec((tm, D_MODEL), lambda i: (i, 0)),
                  pl.BlockSpec((D_MODEL, 4 * D_MODEL), lambda i: (0, 0), pipeline_mode=once),
                  _resident((2, D_MODEL))],
        out_specs=[head_major, head_major, head_major,
                   pl.BlockSpec((tm, D_MODEL), lambda i: (i, 0))],
        compiler_params=_params("parallel"),
        name="hgrn_in_proj",
    )(h, w_in, lb_params)


LEVELS = (32, 16, 8, 4, 2)
SPLIT = 2


def _decay_sum_matrix():
    t = np.arange(CHUNK)[:, None]
    u = np.arange(CHUNK)[None, :]
    mats = [u <= t]
    for m in LEVELS:
        blk = t // m
        odd = (blk % 2) == 1
        q_side = (u >= blk * m) & (u <= t)
        k_side = (u > t) & (u < (blk + 1) * m)
        mats.append(np.where(odd, q_side, k_side))
    d = np.concatenate(mats, axis=0).astype(np.float32)
    return np.concatenate([d] * SPLIT, axis=1)


def _level_index():
    t = np.arange(CHUNK)[:, None]
    s = np.arange(CHUNK)[None, :]
    x = t ^ s
    lvl = np.floor(np.log2(np.maximum(x, 1))).astype(np.int32)
    lvl = np.where(t == s, -1, lvl)
    return np.where(t < s, -2, lvl).astype(np.int32)


def _hgrn_scan_kernel(q_ref, f_ref, v_ref, gate_ref, gn_ref, dmat_ref, lvl_ref, a_ref, st_ref,
                      *, n_chunks):
    @pl.when(pl.program_id(1) == 0)
    def _():
        st_ref[...] = jnp.zeros_like(st_ref)

    pair_w = 2 * HGRN_DK
    row = lax.broadcasted_iota(jnp.int32, (CHUNK, pair_w), 0)
    lvl = lvl_ref[...]
    dmat = dmat_ref[...]
    zero_c = jnp.zeros((CHUNK, HGRN_DK), BF16)
    zero_s = jnp.zeros((HGRN_DK, HGRN_DK), BF16)

    def block_diag(a0, a1, zero):
        return jnp.concatenate([jnp.concatenate([a0, zero], axis=1),
                                jnp.concatenate([zero, a1], axis=1)], axis=0)

    pairs = range(HGRN_HEADS // 2)

    def gram(z):
        zb = block_diag(z[:, :HGRN_DK], z[:, HGRN_DK:], zero_c)
        return lax.dot_general(z, zb, NT_DIMS, preferred_element_type=F32)

    def load_and_sum(j):
        rows = slice(j * CHUNK, (j + 1) * CHUNK)
        out = []
        for p in pairs:
            cat = lambda ref: jnp.concatenate([ref[2 * p, rows, :], ref[2 * p + 1, rows, :]], axis=1)
            qt, fg, v = cat(q_ref).astype(F32), cat(f_ref), cat(v_ref)
            lf = jnp.log2(fg)
            pieces, rest = [], lf
            for _ in range(SPLIT):
                piece = rest.astype(BF16)
                pieces.append(piece)
                rest = rest - piece.astype(F32)
            sums = jnp.dot(dmat, jnp.concatenate(pieces, axis=0), preferred_element_type=F32)
            out.append((qt, fg, v, 1.0 - fg, sums))
        return out

    def scores_and_inter(staged):
        out = []
        for p in pairs:
            qt, fg, _, kk, sums = staged[p]
            qe = (qt * jnp.exp2(sums[0:CHUNK])).astype(BF16)
            st_bd = block_diag(st_ref[2 * p].astype(BF16), st_ref[2 * p + 1].astype(BF16), zero_s)
            o_inter = jnp.dot(qe, st_bd, preferred_element_type=F32)
            scores = jnp.zeros((CHUNK, 2 * CHUNK), F32)
            for li, m in enumerate(LEVELS):
                decay = jnp.exp2(sums[(1 + li) * CHUNK:(2 + li) * CHUNK])
                if m >= 8:
                    blocks = [slice(i * m, (i + 1) * m) for i in range(CHUNK // m)]
                    zq = jnp.concatenate([qt[b] * decay[b] for b in blocks[1::2]], axis=0)
                    zk = jnp.concatenate([kk[b] * decay[b] if i % 2 == 0 else jnp.zeros((m, pair_w), F32)
                                          for i, b in enumerate(blocks)], axis=0).astype(BF16)
                    part = lax.dot_general(zq.astype(BF16),
                                           block_diag(zk[:, :HGRN_DK], zk[:, HGRN_DK:], zero_c),
                                           NT_DIMS, preferred_element_type=F32)
                    zero_rows = jnp.zeros((m, 2 * CHUNK), F32)
                    g = jnp.concatenate([part[(i // 2) * m:(i // 2 + 1) * m] if i % 2 else zero_rows
                                         for i in range(CHUNK // m)], axis=0)
                else:
                    z = jnp.where(((row // m) % 2) == 1, qt, kk) * decay
                    g = gram(z.astype(BF16))
                scores = jnp.where(lvl == int(np.log2(m)), g, scores)
            z = jnp.where((row % 2) == 1, qt * fg, kk).astype(BF16)
            scores = jnp.where(lvl == 0, gram(z), scores)
            kkb = kk.astype(BF16)
            diag = lax.dot_general(qt.astype(BF16), block_diag(kkb[:, :HGRN_DK], kkb[:, HGRN_DK:], zero_c),
                                   NT_DIMS, preferred_element_type=F32)
            out.append((o_inter, jnp.where(lvl == -1, diag, scores)))
        return out

    def output_and_state(j, staged, scored):
        rows = slice(j * CHUNK, (j + 1) * CHUNK)
        outs = []
        for p in pairs:
            _, _, v, kk, sums = staged[p]
            o_inter, scores = scored[p]
            outs.append(o_inter + jnp.dot(scores.astype(BF16),
                                          block_diag(v[:, :HGRN_DK], v[:, HGRN_DK:], zero_c),
                                          preferred_element_type=F32))
            b = sums[0:CHUNK]
            b_last = b[CHUNK - 1:CHUNK, :]
            kd = (kk * jnp.exp2(b_last - b)).astype(BF16)
            dec = jnp.exp2(b_last)
            for hh, sl in ((2 * p, slice(0, HGRN_DK)), (2 * p + 1, slice(HGRN_DK, pair_w))):
                upd = lax.dot_general(kd[:, sl], v[:, sl], TN_DIMS, preferred_element_type=F32)
                dec_rows = jnp.broadcast_to(dec[:, sl], (HGRN_DK, HGRN_DK)).T
                st_ref[hh] = st_ref[hh] * dec_rows + upd
        o_all = jnp.concatenate(outs, axis=1)
        a_ref[rows, :] = (_rmsnorm(o_all, gn_ref[...]) * gate_ref[rows, :].astype(F32)).astype(BF16)

    staged = load_and_sum(0)
    for j in range(n_chunks):
        scored = scores_and_inter(staged)
        nxt = load_and_sum(j + 1) if j + 1 < n_chunks else None
        output_and_state(j, staged, scored)
        staged = nxt


def _hgrn_scan(q, f, v, gate, g_norm, bsz, seq, tc=512):
    n_tok = q.shape[1]
    steps = seq // tc
    heads = pl.BlockSpec((HGRN_HEADS, tc, HGRN_DK), lambda b, c: (0, b * steps + c, 0))
    toks = pl.BlockSpec((tc, D_MODEL), lambda b, c: (b * steps + c, 0))
    dmat = jnp.asarray(_decay_sum_matrix(), BF16)
    lvl = jnp.asarray(np.tile(_level_index(), (1, 2)))
    return pl.pallas_call(
        functools.partial(_hgrn_scan_kernel, n_chunks=tc // CHUNK),
        out_shape=jax.ShapeDtypeStruct((n_tok, D_MODEL), BF16),
        grid=(bsz, steps),
        in_specs=[heads, heads, heads, toks, _resident((1, D_MODEL)), _resident(dmat.shape),
                  _resident(lvl.shape)],
        out_specs=toks,
        scratch_shapes=[pltpu.VMEM((HGRN_HEADS, HGRN_DK, HGRN_DK), F32)],
        compiler_params=_params("parallel", "arbitrary"),
        name="hgrn_scan",
    )(q, f, v, gate, g_norm, dmat, lvl)


def kernel(x, positions, mix_norm, mlp_norm, final_norm, attn_w_qkv, attn_b_qkv, attn_sinks,
           attn_w_o, hgrn_w_in, hgrn_g_norm, hgrn_w_o, hgrn_lower_bounds, mlp_w_up, mlp_w_down):
    bsz, seq, _ = x.shape
    n_tok = bsz * seq
    xf = x.reshape(n_tok, D_MODEL)
    gain = lambda g: g.reshape(1, D_MODEL).astype(F32)

    cos_t, sin_t = _rope_tables(positions)
    qt, k, vt = _qkv_proj(xf, cos_t, sin_t, gain(mix_norm[0]), attn_w_qkv[0].astype(BF16),
                          attn_b_qkv[0].reshape(1, QKV_DIM).astype(F32))
    a = _attention(qt, k, vt, attn_sinks[0].astype(F32), bsz, seq)
    xf, h1 = _out_mlp(a, xf, attn_w_o[0].astype(BF16), gain(mlp_norm[0]), mlp_w_up[0].astype(BF16),
                      mlp_w_down[0].astype(BF16), gain(mix_norm[1]), last=False)

    hq, hf, hv, gate = _hgrn_in(h1, hgrn_w_in[0].astype(BF16), hgrn_lower_bounds.astype(F32))
    a = _hgrn_scan(hq, hf, hv, gate, gain(hgrn_g_norm[0]), bsz, seq)
    out = _out_mlp(a, xf, hgrn_w_o[0].astype(BF16), gain(mlp_norm[1]), mlp_w_up[1].astype(BF16),
                   mlp_w_down[1].astype(BF16), gain(final_norm), last=True)
    return out.reshape(bsz, seq, D_MODEL)
```

```python
import functools

import numpy as np
import jax
import jax.numpy as jnp
from jax import lax
from jax.experimental import pallas as pl
from jax.experimental.pallas import tpu as pltpu

D_MODEL = 1024
HEAD_DIM = 64
N_Q_HEADS = 16
N_KV_HEADS = 4
GROUP = 4
Q_DIM = N_Q_HEADS * HEAD_DIM
KV_DIM = N_KV_HEADS * HEAD_DIM
QKV_DIM = Q_DIM + 2 * KV_DIM
WINDOW = 128
BLOCK = 128
ROT_DIM = 16
ROT_HALF = ROT_DIM // 2
ROPE_THETA = 500000.0
NEG_INF = -1e30
HGRN_HEADS = 8
HGRN_DK = 128
CHUNK = 64
D_FF = 4 * D_MODEL
FF_CHUNK = 1024
AHEAD = 2
NORM_EPS = 1e-5
LOG2E = 1.4426950408889634

LANES = 128
VMEM_LIMIT = 56 * 1024 * 1024

BF16 = jnp.bfloat16
F32 = jnp.float32

NT_DIMS = (((1,), (1,)), ((), ()))
TN_DIMS = (((0,), (0,)), ((), ()))


def _rmsnorm(x, gain):
    ms = jnp.mean(x * x, axis=-1, keepdims=True)
    return x * lax.rsqrt(ms + NORM_EPS) * gain


def _params(*sem):
    return pltpu.CompilerParams(dimension_semantics=sem, vmem_limit_bytes=VMEM_LIMIT)


def _resident(shape):
    return pl.BlockSpec(shape, lambda *_: (0,) * len(shape))


def _cast_specs(weights, n_steps, step_index):
    specs, shapes = [], []
    for w in weights:
        rows = w.shape[0] // n_steps
        specs.append(pl.BlockSpec((rows, w.shape[1]), lambda *ids: (step_index(*ids), 0)))
        shapes.append(jax.ShapeDtypeStruct(w.shape, BF16))
    return specs, shapes


def _cast_slabs(in_refs, out_refs):
    for src, dst in zip(in_refs, out_refs, strict=True):
        dst[...] = src[...].astype(BF16)


def _qkv_kernel(x_ref, pos_ref, invf_ref, g_ref, w_ref, b_ref, qt_ref, k_ref, vt_ref):
    h = _rmsnorm(x_ref[...], g_ref[...]).astype(BF16)
    qkv = jnp.dot(h, w_ref[...], preferred_element_type=F32) + b_ref[...]
    ang = pos_ref[...].astype(F32) * invf_ref[...]
    cos_t = jnp.cos(ang)
    sin_t = jnp.sin(ang)

    def rotate_t(tt):
        parts = []
        for hd in range(LANES // HEAD_DIM):
            base = hd * HEAD_DIM
            x1 = tt[base:base + ROT_HALF]
            x2 = tt[base + ROT_HALF:base + ROT_DIM]
            parts += [x1 * cos_t - x2 * sin_t, x2 * cos_t + x1 * sin_t,
                      tt[base + ROT_DIM:base + HEAD_DIM]]
        return jnp.concatenate(parts, axis=0)

    scale = HEAD_DIM ** -0.5 * LOG2E
    for j in range(Q_DIM // LANES):
        t = qkv[:, j * LANES:(j + 1) * LANES] * scale
        qt_ref[j * LANES:(j + 1) * LANES, :] = rotate_t(t.T).astype(BF16)
    for j in range(KV_DIM // LANES):
        t = qkv[:, Q_DIM + j * LANES:Q_DIM + (j + 1) * LANES]
        k_ref[:, j * LANES:(j + 1) * LANES] = rotate_t(t.T).T.astype(BF16)
        t = qkv[:, Q_DIM + KV_DIM + j * LANES:Q_DIM + KV_DIM + (j + 1) * LANES]
        vt_ref[j * LANES:(j + 1) * LANES, :] = t.T.astype(BF16)


def _qkv_proj(x, positions, gain, w_qkv, b_qkv, tm=1024):
    n_tok = x.shape[0]
    row = lambda w: pl.BlockSpec((tm, w), lambda i: (i, 0))
    col = lambda h: pl.BlockSpec((h, tm), lambda i: (0, i))
    inv_freq = ROPE_THETA ** (-jnp.arange(0, ROT_DIM, 2, dtype=F32) / ROT_DIM)
    return pl.pallas_call(
        _qkv_kernel,
        out_shape=[jax.ShapeDtypeStruct((Q_DIM, n_tok), BF16),
                   jax.ShapeDtypeStruct((n_tok, KV_DIM), BF16),
                   jax.ShapeDtypeStruct((KV_DIM, n_tok), BF16)],
        grid=(n_tok // tm,),
        in_specs=[row(D_MODEL), col(1), _resident((ROT_HALF, 1)), _resident((1, D_MODEL)),
                  _resident((D_MODEL, QKV_DIM)), _resident((1, QKV_DIM))],
        out_specs=[col(Q_DIM), row(KV_DIM), col(KV_DIM)],
        compiler_params=_params("parallel"),
        name="qkv_rope",
    )(x, positions.reshape(1, n_tok), inv_freq.reshape(ROT_HALF, 1), gain, w_qkv, b_qkv)


def _attn_kernel(sink_ref, qt_ref, kp_ref, kc_ref, vtp_ref, vtc_ref, *rest, q_blocks, n_cast):
    cast_in, o_ref, cast_out, bias_ref = rest[:n_cast], rest[n_cast], rest[n_cast + 1:-1], rest[-1]
    _cast_slabs(cast_in, cast_out)
    n = pl.program_id(1)
    band = 2 * BLOCK
    width = GROUP * BLOCK
    kj = lax.broadcasted_iota(jnp.int32, (band, width), 0)
    qi = lax.broadcasted_iota(jnp.int32, (band, width), 1) % BLOCK
    delta = qi + BLOCK - kj
    in_window = (delta >= 0) & (delta < WINDOW)
    bias_ref[0] = jnp.where(in_window & ((kj >= BLOCK) | (n > 0)), 0.0, NEG_INF)
    bias_ref[1] = jnp.where(in_window, 0.0, NEG_INF)
    head_of_lane = lax.broadcasted_iota(jnp.int32, (1, width), 1) // BLOCK
    zero_q = jnp.zeros((HEAD_DIM, width), BF16)
    ones_rows = jnp.ones((16, band), BF16)

    def scores(j, kvh):
        tile = slice((kvh // 2) * LANES, (kvh // 2 + 1) * LANES)
        if j == 0:
            kband = jnp.concatenate([kp_ref[:, tile], kc_ref[:BLOCK, tile]], axis=0)
        else:
            kband = kc_ref[(j - 1) * BLOCK:(j + 1) * BLOCK, tile]
        qc = slice(j * BLOCK, (j + 1) * BLOCK)
        qt4 = jnp.concatenate(
            [qt_ref[(GROUP * kvh + g) * HEAD_DIM:(GROUP * kvh + g + 1) * HEAD_DIM, qc]
             for g in range(GROUP)], axis=1)
        rhs = jnp.concatenate([qt4, zero_q] if kvh % 2 == 0 else [zero_q, qt4], axis=0)
        s = jnp.dot(kband, rhs, preferred_element_type=F32)
        return s + bias_ref[0 if j == 0 else 1]

    def softmax_pv(j, kvh, s):
        hs = slice(kvh * HEAD_DIM, (kvh + 1) * HEAD_DIM)
        if j == 0:
            vt = jnp.concatenate([vtp_ref[hs, :], vtc_ref[hs, :BLOCK]], axis=1)
        else:
            vt = vtc_ref[hs, (j - 1) * BLOCK:(j + 1) * BLOCK]
        sink = jnp.zeros((1, width), F32)
        for g in range(GROUP):
            sink = jnp.where(head_of_lane == g, sink_ref[GROUP * kvh + g] * LOG2E, sink)
        m = jnp.maximum(jnp.max(s, axis=0, keepdims=True), sink)
        e = jnp.exp2(s - m).astype(BF16)
        pv = jnp.dot(jnp.concatenate([vt, ones_rows], axis=0), e,
                     preferred_element_type=F32)
        den = pv[HEAD_DIM:HEAD_DIM + 1, :] + jnp.exp2(sink - m)
        out_t = pv[:HEAD_DIM, :] * (1.0 / den)
        qr = slice(j * BLOCK, (j + 1) * BLOCK)
        for t in range(2):
            pair = jnp.concatenate([out_t[:, (2 * t) * BLOCK:(2 * t + 1) * BLOCK],
                                    out_t[:, (2 * t + 1) * BLOCK:(2 * t + 2) * BLOCK]], axis=0)
            o_ref[qr, (2 * kvh + t) * LANES:(2 * kvh + t + 1) * LANES] = pair.T.astype(BF16)

    items = [(j, kvh) for j in range(q_blocks) for kvh in range(N_KV_HEADS)]
    pending = [scores(*item) for item in items[:AHEAD]]
    for i, item in enumerate(items):
        if i + AHEAD < len(items):
            pending.append(scores(*items[i + AHEAD]))
        softmax_pv(*item, pending.pop(0))


def _attention(qt, k, vt, sinks, bsz, seq, cast_weights, q_blocks=8):
    tq = q_blocks * BLOCK
    steps = seq // tq
    cast_specs, cast_shapes = _cast_specs(cast_weights, bsz * steps, lambda b, n: b * steps + n)
    nb = seq // BLOCK
    cur = lambda b, n: (b * steps + n, 0)
    prev = lambda b, n: (b * nb + jnp.maximum(n * q_blocks - 1, 0), 0)
    cur_t = lambda b, n: (0, b * steps + n)
    prev_t = lambda b, n: (0, b * nb + jnp.maximum(n * q_blocks - 1, 0))
    return pl.pallas_call(
        functools.partial(_attn_kernel, q_blocks=q_blocks, n_cast=len(cast_weights)),
        out_shape=[jax.ShapeDtypeStruct((bsz * seq, Q_DIM), BF16)] + cast_shapes,
        grid=(bsz, steps),
        in_specs=[pl.BlockSpec(memory_space=pltpu.SMEM),
                  pl.BlockSpec((Q_DIM, tq), cur_t),
                  pl.BlockSpec((BLOCK, KV_DIM), prev), pl.BlockSpec((tq, KV_DIM), cur),
                  pl.BlockSpec((KV_DIM, BLOCK), prev_t), pl.BlockSpec((KV_DIM, tq), cur_t)] + cast_specs,
        out_specs=[pl.BlockSpec((tq, Q_DIM), cur)] + cast_specs,
        scratch_shapes=[pltpu.VMEM((2, 2 * BLOCK, GROUP * BLOCK), F32)],
        compiler_params=_params("parallel", "parallel"),
        name="swa_attention",
    )(sinks, qt, k, k, vt, vt, *cast_weights)


def _out_mlp_kernel(a_ref, x_ref, wo_ref, g_ref, wu_ref, wd_ref, gn_ref, *out_refs, last):
    x1 = x_ref[...] + jnp.dot(a_ref[...], wo_ref[...], preferred_element_type=F32)
    h = _rmsnorm(x1, g_ref[...]).astype(BF16)
    x2 = x1
    for c in range(D_FF // FF_CHUNK):
        cols = slice(c * FF_CHUNK, (c + 1) * FF_CHUNK)
        u = jnp.maximum(jnp.dot(h, wu_ref[:, cols], preferred_element_type=F32), 0.0)
        x2 = x2 + jnp.dot((u * u).astype(BF16), wd_ref[cols, :], preferred_element_type=F32)
    normed = _rmsnorm(x2, gn_ref[...])
    if last:
        out_refs[0][...] = normed
    else:
        out_refs[0][...] = x2
        out_refs[1][...] = normed.astype(BF16)


def _out_mlp(a, x, w_o, gain, w_up, w_down, next_gain, last, tm=1024):
    n_tok = x.shape[0]
    row = pl.BlockSpec((tm, D_MODEL), lambda i: (i, 0))
    once = pl.Buffered(1)
    wspec = lambda shape: pl.BlockSpec(shape, lambda i: (0, 0), pipeline_mode=once)
    stream = jax.ShapeDtypeStruct((n_tok, D_MODEL), F32)
    return pl.pallas_call(
        functools.partial(_out_mlp_kernel, last=last),
        out_shape=[stream] if last else [stream, jax.ShapeDtypeStruct((n_tok, D_MODEL), BF16)],
        grid=(n_tok // tm,),
        in_specs=[row, row, wspec((D_MODEL, D_MODEL)), wspec((1, D_MODEL)),
                  wspec((D_MODEL, D_FF)), wspec((D_FF, D_MODEL)), wspec((1, D_MODEL))],
        out_specs=[row] if last else [row, row],
        compiler_params=_params("parallel"),
        name="out_proj_mlp",
    )(a, x, w_o, gain, w_up, w_down, next_gain)


def _hgrn_in_kernel(h_ref, w_ref, lbp_ref, q_ref, f_ref, v_ref, gate_ref):
    h = h_ref[...]
    lbp = lbp_ref[...]
    e = jnp.exp(lbp - jnp.max(lbp, axis=0, keepdims=True))
    sm = e / jnp.sum(e, axis=0, keepdims=True)
    lb = (sm[0:1] + sm[1:2]) - sm[0:1]

    def proj(c):
        return jnp.dot(h, w_ref[:, c * D_MODEL:(c + 1) * D_MODEL], preferred_element_type=F32)

    def by_head(ref, val):
        for hd in range(HGRN_HEADS):
            ref[hd] = val[:, hd * HGRN_DK:(hd + 1) * HGRN_DK]

    q = proj(0)
    f = proj(1)
    by_head(q_ref, (q * jax.nn.sigmoid(q)).astype(BF16))
    g = proj(3)
    by_head(f_ref, lb + (1.0 - lb) * jax.nn.sigmoid(f))
    i = proj(2)
    gate_ref[...] = (g * jax.nn.sigmoid(g)).astype(BF16)
    by_head(v_ref, i.astype(BF16))


def _hgrn_in(h, w_in, lb_params, tm=1024):
    n_tok = h.shape[0]
    head_major = pl.BlockSpec((HGRN_HEADS, tm, HGRN_DK), lambda i: (0, i, 0))
    once = pl.Buffered(1)
    return pl.pallas_call(
        _hgrn_in_kernel,
        out_shape=[jax.ShapeDtypeStruct((HGRN_HEADS, n_tok, HGRN_DK), BF16),
                   jax.ShapeDtypeStruct((HGRN_HEADS, n_tok, HGRN_DK), F32),
                   jax.ShapeDtypeStruct((HGRN_HEADS, n_tok, HGRN_DK), BF16),
                   jax.ShapeDtypeStruct((n_tok, D_MODEL), BF16)],
        grid=(n_tok // tm,),
        in_specs=[pl.BlockSpec((tm, D_MODEL), lambda i: (i, 0)),
                  pl.BlockSpec((D_MODEL, 4 * D_MODEL), lambda i: (0, 0), pipeline_mode=once),
                  _resident((2, D_MODEL))],
        out_specs=[head_major, head_major, head_major,
                   pl.BlockSpec((tm, D_MODEL), lambda i: (i, 0))],
        compiler_params=_params("parallel"),
        name="hgrn_in_proj",
    )(h, w_in, lb_params)


LEVELS = (32, 16, 8, 4, 2)
SPLIT = 2


def _decay_sum_matrix():
    t = np.arange(CHUNK)[:, None]
    u = np.arange(CHUNK)[None, :]
    mats = [u <= t]
    for m in LEVELS:
        blk = t // m
        odd = (blk % 2) == 1
        q_side = (u >= blk * m) & (u <= t)
        k_side = (u > t) & (u < (blk + 1) * m)
        mats.append(np.where(odd, q_side, k_side))
    d = np.concatenate(mats, axis=0).astype(np.float32)
    return np.concatenate([d] * SPLIT, axis=1)


def _level_index():
    t = np.arange(CHUNK)[:, None]
    s = np.arange(CHUNK)[None, :]
    x = t ^ s
    lvl = np.floor(np.log2(np.maximum(x, 1))).astype(np.int32)
    lvl = np.where(t == s, -1, lvl)
    return np.where(t < s, -2, lvl).astype(np.int32)


def _hgrn_scan_kernel(q_ref, f_ref, v_ref, gate_ref, gn_ref, dmat_ref, lvl_ref, a_ref, st_ref,
                      *, n_chunks):
    @pl.when(pl.program_id(1) == 0)
    def _():
        st_ref[...] = jnp.zeros_like(st_ref)

    pair_w = 2 * HGRN_DK
    row = lax.broadcasted_iota(jnp.int32, (CHUNK, pair_w), 0)
    lvl = lvl_ref[...]
    dmat = dmat_ref[...]
    zero_c = jnp.zeros((CHUNK, HGRN_DK), BF16)
    zero_s = jnp.zeros((HGRN_DK, HGRN_DK), BF16)

    def block_diag(a0, a1, zero):
        return jnp.concatenate([jnp.concatenate([a0, zero], axis=1),
                                jnp.concatenate([zero, a1], axis=1)], axis=0)

    pairs = range(HGRN_HEADS // 2)

    def gram(z):
        zb = block_diag(z[:, :HGRN_DK], z[:, HGRN_DK:], zero_c)
        return lax.dot_general(z, zb, NT_DIMS, preferred_element_type=F32)

    def load_and_sum(j):
        rows = slice(j * CHUNK, (j + 1) * CHUNK)
        out = []
        for p in pairs:
            cat = lambda ref: jnp.concatenate([ref[2 * p, rows, :], ref[2 * p + 1, rows, :]], axis=1)
            qt, fg, v = cat(q_ref).astype(F32), cat(f_ref), cat(v_ref)
            lf = jnp.log2(fg)
            pieces, rest = [], lf
            for _ in range(SPLIT):
                piece = rest.astype(BF16)
                pieces.append(piece)
                rest = rest - piece.astype(F32)
            sums = jnp.dot(dmat, jnp.concatenate(pieces, axis=0), preferred_element_type=F32)
            out.append((qt, fg, v, 1.0 - fg, sums))
        return out

    def scores_and_inter(staged):
        out = []
        for p in pairs:
            qt, fg, _, kk, sums = staged[p]
            qe = (qt * jnp.exp2(sums[0:CHUNK])).astype(BF16)
            st_bd = block_diag(st_ref[2 * p].astype(BF16), st_ref[2 * p + 1].astype(BF16), zero_s)
            o_inter = jnp.dot(qe, st_bd, preferred_element_type=F32)
            scores = jnp.zeros((CHUNK, 2 * CHUNK), F32)
            for li, m in enumerate(LEVELS):
                decay = jnp.exp2(sums[(1 + li) * CHUNK:(2 + li) * CHUNK])
                if m >= 8:
                    blocks = [slice(i * m, (i + 1) * m) for i in range(CHUNK // m)]
                    zq = jnp.concatenate([qt[b] * decay[b] for b in blocks[1::2]], axis=0)
                    zk = jnp.concatenate([kk[b] * decay[b] if i % 2 == 0 else jnp.zeros((m, pair_w), F32)
                                          for i, b in enumerate(blocks)], axis=0).astype(BF16)
                    part = lax.dot_general(zq.astype(BF16),
                                           block_diag(zk[:, :HGRN_DK], zk[:, HGRN_DK:], zero_c),
                                           NT_DIMS, preferred_element_type=F32)
                    zero_rows = jnp.zeros((m, 2 * CHUNK), F32)
                    g = jnp.concatenate([part[(i // 2) * m:(i // 2 + 1) * m] if i % 2 else zero_rows
                                         for i in range(CHUNK // m)], axis=0)
                else:
                    z = jnp.where(((row // m) % 2) == 1, qt, kk) * decay
                    g = gram(z.astype(BF16))
                scores = jnp.where(lvl == int(np.log2(m)), g, scores)
            z = jnp.where((row % 2) == 1, qt * fg, kk).astype(BF16)
            scores = jnp.where(lvl == 0, gram(z), scores)
            kkb = kk.astype(BF16)
            diag = lax.dot_general(qt.astype(BF16), block_diag(kkb[:, :HGRN_DK], kkb[:, HGRN_DK:], zero_c),
                                   NT_DIMS, preferred_element_type=F32)
            out.append((o_inter, jnp.where(lvl == -1, diag, scores)))
        return out

    def output_and_state(j, staged, scored):
        rows = slice(j * CHUNK, (j + 1) * CHUNK)
        outs = []
        for p in pairs:
            _, _, v, kk, sums = staged[p]
            o_inter, scores = scored[p]
            outs.append(o_inter + jnp.dot(scores.astype(BF16),
                                          block_diag(v[:, :HGRN_DK], v[:, HGRN_DK:], zero_c),
                                          preferred_element_type=F32))
            b = sums[0:CHUNK]
            b_last = b[CHUNK - 1:CHUNK, :]
            kd = (kk * jnp.exp2(b_last - b)).astype(BF16)
            dec = jnp.exp2(b_last)
            for hh, sl in ((2 * p, slice(0, HGRN_DK)), (2 * p + 1, slice(HGRN_DK, pair_w))):
                upd = lax.dot_general(kd[:, sl], v[:, sl], TN_DIMS, preferred_element_type=F32)
                dec_rows = jnp.broadcast_to(dec[:, sl], (HGRN_DK, HGRN_DK)).T
                st_ref[hh] = st_ref[hh] * dec_rows + upd
        o_all = jnp.concatenate(outs, axis=1)
        a_ref[rows, :] = (_rmsnorm(o_all, gn_ref[...]) * gate_ref[rows, :].astype(F32)).astype(BF16)

    staged = load_and_sum(0)
    for j in range(n_chunks):
        scored = scores_and_inter(staged)
        nxt = load_and_sum(j + 1) if j + 1 < n_chunks else None
        output_and_state(j, staged, scored)
        staged = nxt


def _hgrn_scan(q, f, v, gate, g_norm, bsz, seq, tc=512):
    n_tok = q.shape[1]
    steps = seq // tc
    heads = pl.BlockSpec((HGRN_HEADS, tc, HGRN_DK), lambda b, c: (0, b * steps + c, 0))
    toks = pl.BlockSpec((tc, D_MODEL), lambda b, c: (b * steps + c, 0))
    dmat = jnp.asarray(_decay_sum_matrix(), BF16)
    lvl = jnp.asarray(np.tile(_level_index(), (1, 2)))
    return pl.pallas_call(
        functools.partial(_hgrn_scan_kernel, n_chunks=tc // CHUNK),
        out_shape=jax.ShapeDtypeStruct((n_tok, D_MODEL), BF16),
        grid=(bsz, steps),
        in_specs=[heads, heads, heads, toks, _resident((1, D_MODEL)), _resident(dmat.shape),
                  _resident(lvl.shape)],
        out_specs=toks,
        scratch_shapes=[pltpu.VMEM((HGRN_HEADS, HGRN_DK, HGRN_DK), F32)],
        compiler_params=_params("parallel", "arbitrary"),
        name="hgrn_scan",
    )(q, f, v, gate, g_norm, dmat, lvl)


def kernel(x, positions, mix_norm, mlp_norm, final_norm, attn_w_qkv, attn_b_qkv, attn_sinks,
           attn_w_o, hgrn_w_in, hgrn_g_norm, hgrn_w_o, hgrn_lower_bounds, mlp_w_up, mlp_w_down):
    bsz, seq, _ = x.shape
    n_tok = bsz * seq
    xf = x.reshape(n_tok, D_MODEL)
    gain = lambda g: g.reshape(1, D_MODEL).astype(F32)

    qt, k, vt = _qkv_proj(xf, positions, gain(mix_norm[0]), attn_w_qkv[0].astype(BF16),
                          attn_b_qkv[0].reshape(1, QKV_DIM).astype(F32))
    a, w_o0, w_up0, w_down0, w_in, w_o1, w_up1, w_down1 = _attention(
        qt, k, vt, attn_sinks[0].astype(F32), bsz, seq,
        (attn_w_o[0], mlp_w_up[0], mlp_w_down[0], hgrn_w_in[0], hgrn_w_o[0], mlp_w_up[1], mlp_w_down[1]))
    xf, h1 = _out_mlp(a, xf, w_o0, gain(mlp_norm[0]), w_up0, w_down0, gain(mix_norm[1]), last=False)

    hq, hf, hv, gate = _hgrn_in(h1, w_in, hgrn_lower_bounds.astype(F32))
    a = _hgrn_scan(hq, hf, hv, gate, gain(hgrn_g_norm[0]), bsz, seq)
    (out,) = _out_mlp(a, xf, w_o1, gain(mlp_norm[1]), w_up1, w_down1, gain(final_norm), last=True)
    return out.reshape(bsz, seq, D_MODEL)
```

```python
import functools

import numpy as np
import jax
import jax.numpy as jnp
from jax import lax
from jax.experimental import pallas as pl
from jax.experimental.pallas import tpu as pltpu

D_MODEL = 1024
HEAD_DIM = 64
N_Q_HEADS = 16
N_KV_HEADS = 4
GROUP = 4
Q_DIM = N_Q_HEADS * HEAD_DIM
KV_DIM = N_KV_HEADS * HEAD_DIM
QKV_DIM = Q_DIM + 2 * KV_DIM
WINDOW = 128
BLOCK = 128
ROT_DIM = 16
ROT_HALF = ROT_DIM // 2
ROPE_THETA = 500000.0
NEG_INF = -1e30
HGRN_HEADS = 8
HGRN_DK = 128
CHUNK = 64
D_FF = 4 * D_MODEL
FF_CHUNK = 1024
AHEAD = 2
NORM_EPS = 1e-5
LOG2E = 1.4426950408889634

LANES = 128
VMEM_LIMIT = 56 * 1024 * 1024

BF16 = jnp.bfloat16
F32 = jnp.float32

NT_DIMS = (((1,), (1,)), ((), ()))
TN_DIMS = (((0,), (0,)), ((), ()))


def _rmsnorm(x, gain):
    ms = jnp.mean(x * x, axis=-1, keepdims=True)
    return x * lax.rsqrt(ms + NORM_EPS) * gain


def _params(*sem):
    return pltpu.CompilerParams(dimension_semantics=sem, vmem_limit_bytes=VMEM_LIMIT)


def _resident(shape):
    return pl.BlockSpec(shape, lambda *_: (0,) * len(shape))


def _cast_specs(weights, n_steps, step_index):
    in_specs, out_specs, shapes = [], [], []
    for w, layer in weights:
        _, n_rows, n_cols = w.shape
        rows = n_rows // n_steps
        in_specs.append(pl.BlockSpec((None, rows, n_cols),
                                     lambda *ids, layer=layer: (layer, step_index(*ids), 0)))
        out_specs.append(pl.BlockSpec((rows, n_cols), lambda *ids: (step_index(*ids), 0)))
        shapes.append(jax.ShapeDtypeStruct((n_rows, n_cols), BF16))
    return in_specs, out_specs, shapes


def _cast_slabs(in_refs, out_refs):
    for src, dst in zip(in_refs, out_refs, strict=True):
        dst[...] = src[...].astype(BF16)


def _qkv_kernel(x_ref, pos_ref, invf_ref, g_ref, w_ref, b_ref, qt_ref, k_ref, vt_ref):
    h = _rmsnorm(x_ref[...], g_ref[...]).astype(BF16)
    qkv = jnp.dot(h, w_ref[...], preferred_element_type=F32) + b_ref[...]
    ang = pos_ref[...].astype(F32) * invf_ref[...]
    cos_t = jnp.cos(ang)
    sin_t = jnp.sin(ang)

    def rotate_t(tt):
        parts = []
        for hd in range(LANES // HEAD_DIM):
            base = hd * HEAD_DIM
            x1 = tt[base:base + ROT_HALF]
            x2 = tt[base + ROT_HALF:base + ROT_DIM]
            parts += [x1 * cos_t - x2 * sin_t, x2 * cos_t + x1 * sin_t,
                      tt[base + ROT_DIM:base + HEAD_DIM]]
        return jnp.concatenate(parts, axis=0)

    scale = HEAD_DIM ** -0.5 * LOG2E
    for j in range(Q_DIM // LANES):
        t = qkv[:, j * LANES:(j + 1) * LANES] * scale
        qt_ref[j * LANES:(j + 1) * LANES, :] = rotate_t(t.T).astype(BF16)
    for j in range(KV_DIM // LANES):
        t = qkv[:, Q_DIM + j * LANES:Q_DIM + (j + 1) * LANES]
        k_ref[:, j * LANES:(j + 1) * LANES] = rotate_t(t.T).T.astype(BF16)
        t = qkv[:, Q_DIM + KV_DIM + j * LANES:Q_DIM + KV_DIM + (j + 1) * LANES]
        vt_ref[j * LANES:(j + 1) * LANES, :] = t.T.astype(BF16)


def _qkv_proj(x, positions, gain, w_qkv, b_qkv, tm=1024):
    n_tok = x.shape[0]
    row = lambda w: pl.BlockSpec((tm, w), lambda i: (i, 0))
    col = lambda h: pl.BlockSpec((h, tm), lambda i: (0, i))
    inv_freq = ROPE_THETA ** (-jnp.arange(0, ROT_DIM, 2, dtype=F32) / ROT_DIM)
    return pl.pallas_call(
        _qkv_kernel,
        out_shape=[jax.ShapeDtypeStruct((Q_DIM, n_tok), BF16),
                   jax.ShapeDtypeStruct((n_tok, KV_DIM), BF16),
                   jax.ShapeDtypeStruct((KV_DIM, n_tok), BF16)],
        grid=(n_tok // tm,),
        in_specs=[row(D_MODEL), col(1), _resident((ROT_HALF, 1)), _resident((1, D_MODEL)),
                  _resident((D_MODEL, QKV_DIM)), _resident((1, QKV_DIM))],
        out_specs=[col(Q_DIM), row(KV_DIM), col(KV_DIM)],
        compiler_params=_params("parallel"),
        name="qkv_rope",
    )(x, positions.reshape(1, n_tok), inv_freq.reshape(ROT_HALF, 1), gain, w_qkv, b_qkv)


def _attn_kernel(sink_ref, qt_ref, kp_ref, kc_ref, vtp_ref, vtc_ref, *rest, q_blocks, n_cast):
    cast_in, o_ref, cast_out, bias_ref = rest[:n_cast], rest[n_cast], rest[n_cast + 1:-1], rest[-1]
    _cast_slabs(cast_in, cast_out)
    n = pl.program_id(1)
    band = 2 * BLOCK
    width = GROUP * BLOCK
    kj = lax.broadcasted_iota(jnp.int32, (band, width), 0)
    qi = lax.broadcasted_iota(jnp.int32, (band, width), 1) % BLOCK
    delta = qi + BLOCK - kj
    in_window = (delta >= 0) & (delta < WINDOW)
    bias_ref[0] = jnp.where(in_window & ((kj >= BLOCK) | (n > 0)), 0.0, NEG_INF)
    bias_ref[1] = jnp.where(in_window, 0.0, NEG_INF)
    head_of_lane = lax.broadcasted_iota(jnp.int32, (1, width), 1) // BLOCK
    zero_q = jnp.zeros((HEAD_DIM, width), BF16)
    ones_rows = jnp.ones((16, band), BF16)

    def scores(j, kvh):
        tile = slice((kvh // 2) * LANES, (kvh // 2 + 1) * LANES)
        if j == 0:
            kband = jnp.concatenate([kp_ref[:, tile], kc_ref[:BLOCK, tile]], axis=0)
        else:
            kband = kc_ref[(j - 1) * BLOCK:(j + 1) * BLOCK, tile]
        qc = slice(j * BLOCK, (j + 1) * BLOCK)
        qt4 = jnp.concatenate(
            [qt_ref[(GROUP * kvh + g) * HEAD_DIM:(GROUP * kvh + g + 1) * HEAD_DIM, qc]
             for g in range(GROUP)], axis=1)
        rhs = jnp.concatenate([qt4, zero_q] if kvh % 2 == 0 else [zero_q, qt4], axis=0)
        s = jnp.dot(kband, rhs, preferred_element_type=F32)
        return s + bias_ref[0 if j == 0 else 1]

    def softmax_pv(j, kvh, s):
        hs = slice(kvh * HEAD_DIM, (kvh + 1) * HEAD_DIM)
        if j == 0:
            vt = jnp.concatenate([vtp_ref[hs, :], vtc_ref[hs, :BLOCK]], axis=1)
        else:
            vt = vtc_ref[hs, (j - 1) * BLOCK:(j + 1) * BLOCK]
        sink = jnp.zeros((1, width), F32)
        for g in range(GROUP):
            sink = jnp.where(head_of_lane == g, sink_ref[GROUP * kvh + g] * LOG2E, sink)
        m = jnp.maximum(jnp.max(s, axis=0, keepdims=True), sink)
        e = jnp.exp2(s - m).astype(BF16)
        pv = jnp.dot(jnp.concatenate([vt, ones_rows], axis=0), e,
                     preferred_element_type=F32)
        den = pv[HEAD_DIM:HEAD_DIM + 1, :] + jnp.exp2(sink - m)
        out_t = pv[:HEAD_DIM, :] * (1.0 / den)
        qr = slice(j * BLOCK, (j + 1) * BLOCK)
        for t in range(2):
            pair = jnp.concatenate([out_t[:, (2 * t) * BLOCK:(2 * t + 1) * BLOCK],
                                    out_t[:, (2 * t + 1) * BLOCK:(2 * t + 2) * BLOCK]], axis=0)
            o_ref[qr, (2 * kvh + t) * LANES:(2 * kvh + t + 1) * LANES] = pair.T.astype(BF16)

    items = [(j, kvh) for j in range(q_blocks) for kvh in range(N_KV_HEADS)]
    pending = [scores(*item) for item in items[:AHEAD]]
    for i, item in enumerate(items):
        if i + AHEAD < len(items):
            pending.append(scores(*items[i + AHEAD]))
        softmax_pv(*item, pending.pop(0))


def _attention(qt, k, vt, sinks, bsz, seq, cast_weights, q_blocks=8):
    tq = q_blocks * BLOCK
    steps = seq // tq
    cast_in, cast_out, cast_shapes = _cast_specs(cast_weights, bsz * steps, lambda b, n: b * steps + n)
    nb = seq // BLOCK
    cur = lambda b, n: (b * steps + n, 0)
    prev = lambda b, n: (b * nb + jnp.maximum(n * q_blocks - 1, 0), 0)
    cur_t = lambda b, n: (0, b * steps + n)
    prev_t = lambda b, n: (0, b * nb + jnp.maximum(n * q_blocks - 1, 0))
    return pl.pallas_call(
        functools.partial(_attn_kernel, q_blocks=q_blocks, n_cast=len(cast_weights)),
        out_shape=[jax.ShapeDtypeStruct((bsz * seq, Q_DIM), BF16)] + cast_shapes,
        grid=(bsz, steps),
        in_specs=[pl.BlockSpec(memory_space=pltpu.SMEM),
                  pl.BlockSpec((Q_DIM, tq), cur_t),
                  pl.BlockSpec((BLOCK, KV_DIM), prev), pl.BlockSpec((tq, KV_DIM), cur),
                  pl.BlockSpec((KV_DIM, BLOCK), prev_t), pl.BlockSpec((KV_DIM, tq), cur_t)] + cast_in,
        out_specs=[pl.BlockSpec((tq, Q_DIM), cur)] + cast_out,
        scratch_shapes=[pltpu.VMEM((2, 2 * BLOCK, GROUP * BLOCK), F32)],
        compiler_params=_params("parallel", "parallel"),
        name="swa_attention",
    )(sinks, qt, k, k, vt, vt, *[w for w, _ in cast_weights])


def _out_mlp_kernel(a_ref, x_ref, wo_ref, g_ref, wu_ref, wd_ref, gn_ref, *out_refs, last):
    x1 = x_ref[...] + jnp.dot(a_ref[...], wo_ref[...], preferred_element_type=F32)
    h = _rmsnorm(x1, g_ref[...]).astype(BF16)
    x2 = x1
    for c in range(D_FF // FF_CHUNK):
        cols = slice(c * FF_CHUNK, (c + 1) * FF_CHUNK)
        u = jnp.maximum(jnp.dot(h, wu_ref[:, cols], preferred_element_type=F32), 0.0)
        x2 = x2 + jnp.dot((u * u).astype(BF16), wd_ref[cols, :], preferred_element_type=F32)
    normed = _rmsnorm(x2, gn_ref[...])
    if last:
        out_refs[0][...] = normed
    else:
        out_refs[0][...] = x2
        out_refs[1][...] = normed.astype(BF16)


def _out_mlp(a, x, w_o, gain, w_up, w_down, next_gain, last, tm=1024):
    n_tok = x.shape[0]
    row = pl.BlockSpec((tm, D_MODEL), lambda i: (i, 0))
    once = pl.Buffered(1)
    wspec = lambda shape: pl.BlockSpec(shape, lambda i: (0, 0), pipeline_mode=once)
    stream = jax.ShapeDtypeStruct((n_tok, D_MODEL), F32)
    return pl.pallas_call(
        functools.partial(_out_mlp_kernel, last=last),
        out_shape=[stream] if last else [stream, jax.ShapeDtypeStruct((n_tok, D_MODEL), BF16)],
        grid=(n_tok // tm,),
        in_specs=[row, row, wspec((D_MODEL, D_MODEL)), wspec((1, D_MODEL)),
                  wspec((D_MODEL, D_FF)), wspec((D_FF, D_MODEL)), wspec((1, D_MODEL))],
        out_specs=[row] if last else [row, row],
        compiler_params=_params("parallel"),
        name="out_proj_mlp",
    )(a, x, w_o, gain, w_up, w_down, next_gain)


def _hgrn_in_kernel(h_ref, w_ref, lbp_ref, q_ref, f_ref, v_ref, gate_ref):
    h = h_ref[...]
    lbp = lbp_ref[...]
    e = jnp.exp(lbp - jnp.max(lbp, axis=0, keepdims=True))
    sm = e / jnp.sum(e, axis=0, keepdims=True)
    lb = (sm[0:1] + sm[1:2]) - sm[0:1]

    def proj(c):
        return jnp.dot(h, w_ref[:, c * D_MODEL:(c + 1) * D_MODEL], preferred_element_type=F32)

    def by_head(ref, val):
        for hd in range(HGRN_HEADS):
            ref[hd] = val[:, hd * HGRN_DK:(hd + 1) * HGRN_DK]

    q = proj(0)
    f = proj(1)
    by_head(q_ref, (q * jax.nn.sigmoid(q)).astype(BF16))
    g = proj(3)
    by_head(f_ref, lb + (1.0 - lb) * jax.nn.sigmoid(f))
    i = proj(2)
    gate_ref[...] = (g * jax.nn.sigmoid(g)).astype(BF16)
    by_head(v_ref, i.astype(BF16))


def _hgrn_in(h, w_in, lb_params, tm=1024):
    n_tok = h.shape[0]
    head_major = pl.BlockSpec((HGRN_HEADS, tm, HGRN_DK), lambda i: (0, i, 0))
    once = pl.Buffered(1)
    return pl.pallas_call(
        _hgrn_in_kernel,
        out_shape=[jax.ShapeDtypeStruct((HGRN_HEADS, n_tok, HGRN_DK), BF16),
                   jax.ShapeDtypeStruct((HGRN_HEADS, n_tok, HGRN_DK), F32),
                   jax.ShapeDtypeStruct((HGRN_HEADS, n_tok, HGRN_DK), BF16),
                   jax.ShapeDtypeStruct((n_tok, D_MODEL), BF16)],
        grid=(n_tok // tm,),
        in_specs=[pl.BlockSpec((tm, D_MODEL), lambda i: (i, 0)),
                  pl.BlockSpec((D_MODEL, 4 * D_MODEL), lambda i: (0, 0), pipeline_mode=once),
                  _resident((2, D_MODEL))],
        out_specs=[head_major, head_major, head_major,
                   pl.BlockSpec((tm, D_MODEL), lambda i: (i, 0))],
        compiler_params=_params("parallel"),
        name="hgrn_in_proj",
    )(h, w_in, lb_params)


LEVELS = (32, 16, 8, 4, 2)
SPLIT = 2


def _decay_sum_matrix():
    t = np.arange(CHUNK)[:, None]
    u = np.arange(CHUNK)[None, :]
    mats = [u <= t]
    for m in LEVELS:
        blk = t // m
        odd = (blk % 2) == 1
        q_side = (u >= blk * m) & (u <= t)
        k_side = (u > t) & (u < (blk + 1) * m)
        mats.append(np.where(odd, q_side, k_side))
    d = np.concatenate(mats, axis=0).astype(np.float32)
    return np.concatenate([d] * SPLIT, axis=1)


def _level_index():
    t = np.arange(CHUNK)[:, None]
    s = np.arange(CHUNK)[None, :]
    x = t ^ s
    lvl = np.floor(np.log2(np.maximum(x, 1))).astype(np.int32)
    lvl = np.where(t == s, -1, lvl)
    return np.where(t < s, -2, lvl).astype(np.int32)


def _hgrn_scan_kernel(q_ref, f_ref, v_ref, gate_ref, gn_ref, dmat_ref, lvl_ref, a_ref, st_ref,
                      *, n_chunks):
    @pl.when(pl.program_id(1) == 0)
    def _():
        st_ref[...] = jnp.zeros_like(st_ref)

    pair_w = 2 * HGRN_DK
    row = lax.broadcasted_iota(jnp.int32, (CHUNK, pair_w), 0)
    lvl = lvl_ref[...]
    dmat = dmat_ref[...]
    zero_c = jnp.zeros((CHUNK, HGRN_DK), BF16)
    zero_s = jnp.zeros((HGRN_DK, HGRN_DK), BF16)

    def block_diag(a0, a1, zero):
        return jnp.concatenate([jnp.concatenate([a0, zero], axis=1),
                                jnp.concatenate([zero, a1], axis=1)], axis=0)

    pairs = range(HGRN_HEADS // 2)

    def gram(z):
        zb = block_diag(z[:, :HGRN_DK], z[:, HGRN_DK:], zero_c)
        return lax.dot_general(z, zb, NT_DIMS, preferred_element_type=F32)

    def load_and_sum(j):
        rows = slice(j * CHUNK, (j + 1) * CHUNK)
        out = []
        for p in pairs:
            cat = lambda ref: jnp.concatenate([ref[2 * p, rows, :], ref[2 * p + 1, rows, :]], axis=1)
            qt, fg, v = cat(q_ref).astype(F32), cat(f_ref), cat(v_ref)
            lf = jnp.log2(fg)
            pieces, rest = [], lf
            for _ in range(SPLIT):
                piece = rest.astype(BF16)
                pieces.append(piece)
                rest = rest - piece.astype(F32)
            sums = jnp.dot(dmat, jnp.concatenate(pieces, axis=0), preferred_element_type=F32)
            out.append((qt, fg, v, 1.0 - fg, sums))
        return out

    def scores_and_inter(staged):
        out = []
        for p in pairs:
            qt, fg, _, kk, sums = staged[p]
            qe = (qt * jnp.exp2(sums[0:CHUNK])).astype(BF16)
            st_bd = block_diag(st_ref[2 * p].astype(BF16), st_ref[2 * p + 1].astype(BF16), zero_s)
            o_inter = jnp.dot(qe, st_bd, preferred_element_type=F32)
            scores = jnp.zeros((CHUNK, 2 * CHUNK), F32)
            for li, m in enumerate(LEVELS):
                decay = jnp.exp2(sums[(1 + li) * CHUNK:(2 + li) * CHUNK])
                if m >= 8:
                    blocks = [slice(i * m, (i + 1) * m) for i in range(CHUNK // m)]
                    zq = jnp.concatenate([qt[b] * decay[b] for b in blocks[1::2]], axis=0)
                    zk = jnp.concatenate([kk[b] * decay[b] if i % 2 == 0 else jnp.zeros((m, pair_w), F32)
                                          for i, b in enumerate(blocks)], axis=0).astype(BF16)
                    part = lax.dot_general(zq.astype(BF16),
                                           block_diag(zk[:, :HGRN_DK], zk[:, HGRN_DK:], zero_c),
                                           NT_DIMS, preferred_element_type=F32)
                    zero_rows = jnp.zeros((m, 2 * CHUNK), F32)
                    g = jnp.concatenate([part[(i // 2) * m:(i // 2 + 1) * m] if i % 2 else zero_rows
                                         for i in range(CHUNK // m)], axis=0)
                else:
                    z = jnp.where(((row // m) % 2) == 1, qt, kk) * decay
                    g = gram(z.astype(BF16))
                scores = jnp.where(lvl == int(np.log2(m)), g, scores)
            z = jnp.where((row % 2) == 1, qt * fg, kk).astype(BF16)
            scores = jnp.where(lvl == 0, gram(z), scores)
            kkb = kk.astype(BF16)
            diag = lax.dot_general(qt.astype(BF16), block_diag(kkb[:, :HGRN_DK], kkb[:, HGRN_DK:], zero_c),
                                   NT_DIMS, preferred_element_type=F32)
            out.append((o_inter, jnp.where(lvl == -1, diag, scores)))
        return out

    def output_and_state(j, staged, scored):
        rows = slice(j * CHUNK, (j + 1) * CHUNK)
        outs = []
        for p in pairs:
            _, _, v, kk, sums = staged[p]
            o_inter, scores = scored[p]
            outs.append(o_inter + jnp.dot(scores.astype(BF16),
                                          block_diag(v[:, :HGRN_DK], v[:, HGRN_DK:], zero_c),
                                          preferred_element_type=F32))
            b = sums[0:CHUNK]
            b_last = b[CHUNK - 1:CHUNK, :]
            kd = (kk * jnp.exp2(b_last - b)).astype(BF16)
            dec = jnp.exp2(b_last)
            for hh, sl in ((2 * p, slice(0, HGRN_DK)), (2 * p + 1, slice(HGRN_DK, pair_w))):
                upd = lax.dot_general(kd[:, sl], v[:, sl], TN_DIMS, preferred_element_type=F32)
                dec_rows = jnp.broadcast_to(dec[:, sl], (HGRN_DK, HGRN_DK)).T
                st_ref[hh] = st_ref[hh] * dec_rows + upd
        o_all = jnp.concatenate(outs, axis=1)
        a_ref[rows, :] = (_rmsnorm(o_all, gn_ref[...]) * gate_ref[rows, :].astype(F32)).astype(BF16)

    staged = load_and_sum(0)
    for j in range(n_chunks):
        scored = scores_and_inter(staged)
        nxt = load_and_sum(j + 1) if j + 1 < n_chunks else None
        output_and_state(j, staged, scored)
        staged = nxt


def _hgrn_scan(q, f, v, gate, g_norm, bsz, seq, tc=512):
    n_tok = q.shape[1]
    steps = seq // tc
    heads = pl.BlockSpec((HGRN_HEADS, tc, HGRN_DK), lambda b, c: (0, b * steps + c, 0))
    toks = pl.BlockSpec((tc, D_MODEL), lambda b, c: (b * steps + c, 0))
    dmat = jnp.asarray(_decay_sum_matrix(), BF16)
    lvl = jnp.asarray(np.tile(_level_index(), (1, 2)))
    return pl.pallas_call(
        functools.partial(_hgrn_scan_kernel, n_chunks=tc // CHUNK),
        out_shape=jax.ShapeDtypeStruct((n_tok, D_MODEL), BF16),
        grid=(bsz, steps),
        in_specs=[heads, heads, heads, toks, _resident((1, D_MODEL)), _resident(dmat.shape),
                  _resident(lvl.shape)],
        out_specs=toks,
        scratch_shapes=[pltpu.VMEM((HGRN_HEADS, HGRN_DK, HGRN_DK), F32)],
        compiler_params=_params("parallel", "arbitrary"),
        name="hgrn_scan",
    )(q, f, v, gate, g_norm, dmat, lvl)


def kernel(x, positions, mix_norm, mlp_norm, final_norm, attn_w_qkv, attn_b_qkv, attn_sinks,
           attn_w_o, hgrn_w_in, hgrn_g_norm, hgrn_w_o, hgrn_lower_bounds, mlp_w_up, mlp_w_down):
    bsz, seq, _ = x.shape
    n_tok = bsz * seq
    xf = x.reshape(n_tok, D_MODEL)
    gain = lambda g: g.reshape(1, D_MODEL).astype(F32)

    qt, k, vt = _qkv_proj(xf, positions, gain(mix_norm[0]), attn_w_qkv[0].astype(BF16),
                          attn_b_qkv[0].reshape(1, QKV_DIM).astype(F32))
    a, w_o0, w_up0, w_down0, w_in, w_o1, w_up1, w_down1 = _attention(
        qt, k, vt, attn_sinks[0].astype(F32), bsz, seq,
        [(attn_w_o, 0), (mlp_w_up, 0), (mlp_w_down, 0), (hgrn_w_in, 0), (hgrn_w_o, 0),
         (mlp_w_up, 1), (mlp_w_down, 1)])
    xf, h1 = _out_mlp(a, xf, w_o0, gain(mlp_norm[0]), w_up0, w_down0, gain(mix_norm[1]), last=False)

    hq, hf, hv, gate = _hgrn_in(h1, w_in, hgrn_lower_bounds.astype(F32))
    a = _hgrn_scan(hq, hf, hv, gate, gain(hgrn_g_norm[0]), bsz, seq)
    (out,) = _out_mlp(a, xf, w_o1, gain(mlp_norm[1]), w_up1, w_down1, gain(final_norm), last=True)
    return out.reshape(bsz, seq, D_MODEL)
```

```python
import functools

import numpy as np
import jax
import jax.numpy as jnp
from jax import lax
from jax.experimental import pallas as pl
from jax.experimental.pallas import tpu as pltpu

D_MODEL = 1024
HEAD_DIM = 64
N_Q_HEADS = 16
N_KV_HEADS = 4
GROUP = 4
Q_DIM = N_Q_HEADS * HEAD_DIM
KV_DIM = N_KV_HEADS * HEAD_DIM
QKV_DIM = Q_DIM + 2 * KV_DIM
WINDOW = 128
BLOCK = 128
ROT_DIM = 16
ROT_HALF = ROT_DIM // 2
ROPE_THETA = 500000.0
NEG_INF = -1e30
HGRN_HEADS = 8
HGRN_DK = 128
CHUNK = 64
D_FF = 4 * D_MODEL
FF_CHUNK = 1024
AHEAD = 2
NORM_EPS = 1e-5
LOG2E = 1.4426950408889634

LANES = 128
VMEM_LIMIT = 56 * 1024 * 1024

BF16 = jnp.bfloat16
F32 = jnp.float32

NT_DIMS = (((1,), (1,)), ((), ()))
TN_DIMS = (((0,), (0,)), ((), ()))


def _rmsnorm(x, gain):
    ms = jnp.mean(x * x, axis=-1, keepdims=True)
    return x * lax.rsqrt(ms + NORM_EPS) * gain


def _params(*sem):
    return pltpu.CompilerParams(dimension_semantics=sem, vmem_limit_bytes=VMEM_LIMIT)


def _resident(shape):
    return pl.BlockSpec(shape, lambda *_: (0,) * len(shape))


def _cast_specs(weights, n_steps, step_index):
    in_specs, out_specs, shapes = [], [], []
    for w, layer in weights:
        _, n_rows, n_cols = w.shape
        rows = n_rows // n_steps
        in_specs.append(pl.BlockSpec((None, rows, n_cols),
                                     lambda *ids, layer=layer: (layer, step_index(*ids), 0)))
        out_specs.append(pl.BlockSpec((rows, n_cols), lambda *ids: (step_index(*ids), 0)))
        shapes.append(jax.ShapeDtypeStruct((n_rows, n_cols), BF16))
    return in_specs, out_specs, shapes


def _cast_slabs(in_refs, out_refs):
    for src, dst in zip(in_refs, out_refs, strict=True):
        dst[...] = src[...].astype(BF16)


def _qkv_kernel(x_ref, pos_ref, invf_ref, g_ref, w_ref, b_ref, qt_ref, k_ref, vt_ref):
    h = _rmsnorm(x_ref[...], g_ref[...]).astype(BF16)
    qkv = jnp.dot(h, w_ref[...], preferred_element_type=F32) + b_ref[...]
    ang = pos_ref[...].astype(F32) * invf_ref[...]
    cos_t = jnp.cos(ang)
    sin_t = jnp.sin(ang)

    def rotate_t(tt):
        parts = []
        for hd in range(LANES // HEAD_DIM):
            base = hd * HEAD_DIM
            x1 = tt[base:base + ROT_HALF]
            x2 = tt[base + ROT_HALF:base + ROT_DIM]
            parts += [x1 * cos_t - x2 * sin_t, x2 * cos_t + x1 * sin_t,
                      tt[base + ROT_DIM:base + HEAD_DIM]]
        return jnp.concatenate(parts, axis=0)

    scale = HEAD_DIM ** -0.5 * LOG2E
    for j in range(Q_DIM // LANES):
        t = qkv[:, j * LANES:(j + 1) * LANES] * scale
        qt_ref[j * LANES:(j + 1) * LANES, :] = rotate_t(t.T).astype(BF16)
    for j in range(KV_DIM // LANES):
        t = qkv[:, Q_DIM + j * LANES:Q_DIM + (j + 1) * LANES]
        k_ref[:, j * LANES:(j + 1) * LANES] = rotate_t(t.T).T.astype(BF16)
        t = qkv[:, Q_DIM + KV_DIM + j * LANES:Q_DIM + KV_DIM + (j + 1) * LANES]
        vt_ref[j * LANES:(j + 1) * LANES, :] = t.T.astype(BF16)


def _qkv_proj(x, positions, gain, w_qkv, b_qkv, tm=1024):
    n_tok = x.shape[0]
    row = lambda w: pl.BlockSpec((tm, w), lambda i: (i, 0))
    col = lambda h: pl.BlockSpec((h, tm), lambda i: (0, i))
    inv_freq = ROPE_THETA ** (-jnp.arange(0, ROT_DIM, 2, dtype=F32) / ROT_DIM)
    return pl.pallas_call(
        _qkv_kernel,
        out_shape=[jax.ShapeDtypeStruct((Q_DIM, n_tok), BF16),
                   jax.ShapeDtypeStruct((n_tok, KV_DIM), BF16),
                   jax.ShapeDtypeStruct((KV_DIM, n_tok), BF16)],
        grid=(n_tok // tm,),
        in_specs=[row(D_MODEL), col(1), _resident((ROT_HALF, 1)), _resident((1, D_MODEL)),
                  _resident((D_MODEL, QKV_DIM)), _resident((1, QKV_DIM))],
        out_specs=[col(Q_DIM), row(KV_DIM), col(KV_DIM)],
        compiler_params=_params("parallel"),
        name="qkv_rope",
    )(x, positions.reshape(1, n_tok), inv_freq.reshape(ROT_HALF, 1), gain, w_qkv, b_qkv)


def _attn_kernel(sink_ref, qt_ref, kp_ref, kc_ref, vtp_ref, vtc_ref, *rest, q_blocks, n_cast):
    cast_in, o_ref, cast_out, bias_ref = rest[:n_cast], rest[n_cast], rest[n_cast + 1:-1], rest[-1]
    _cast_slabs(cast_in, cast_out)
    n = pl.program_id(1)
    band = 2 * BLOCK
    width = GROUP * BLOCK
    kj = lax.broadcasted_iota(jnp.int32, (band, width), 0)
    qi = lax.broadcasted_iota(jnp.int32, (band, width), 1) % BLOCK
    delta = qi + BLOCK - kj
    in_window = (delta >= 0) & (delta < WINDOW)
    bias_ref[0] = jnp.where(in_window & ((kj >= BLOCK) | (n > 0)), 0.0, NEG_INF)
    bias_ref[1] = jnp.where(in_window, 0.0, NEG_INF)
    head_of_lane = lax.broadcasted_iota(jnp.int32, (1, width), 1) // BLOCK
    zero_q = jnp.zeros((HEAD_DIM, width), BF16)
    ones_rows = jnp.ones((16, band), BF16)

    def scores(j, kvh):
        tile = slice((kvh // 2) * LANES, (kvh // 2 + 1) * LANES)
        if j == 0:
            kband = jnp.concatenate([kp_ref[:, tile], kc_ref[:BLOCK, tile]], axis=0)
        else:
            kband = kc_ref[(j - 1) * BLOCK:(j + 1) * BLOCK, tile]
        qc = slice(j * BLOCK, (j + 1) * BLOCK)
        qt4 = jnp.concatenate(
            [qt_ref[(GROUP * kvh + g) * HEAD_DIM:(GROUP * kvh + g + 1) * HEAD_DIM, qc]
             for g in range(GROUP)], axis=1)
        rhs = jnp.concatenate([qt4, zero_q] if kvh % 2 == 0 else [zero_q, qt4], axis=0)
        s = jnp.dot(kband, rhs, preferred_element_type=F32)
        return s + bias_ref[0 if j == 0 else 1]

    def softmax_pv(j, kvh, s):
        hs = slice(kvh * HEAD_DIM, (kvh + 1) * HEAD_DIM)
        if j == 0:
            vt = jnp.concatenate([vtp_ref[hs, :], vtc_ref[hs, :BLOCK]], axis=1)
        else:
            vt = vtc_ref[hs, (j - 1) * BLOCK:(j + 1) * BLOCK]
        sink = jnp.zeros((1, width), F32)
        for g in range(GROUP):
            sink = jnp.where(head_of_lane == g, sink_ref[GROUP * kvh + g] * LOG2E, sink)
        m = jnp.maximum(jnp.max(s, axis=0, keepdims=True), sink)
        e = jnp.exp2(s - m).astype(BF16)
        pv = jnp.dot(jnp.concatenate([vt, ones_rows], axis=0), e,
                     preferred_element_type=F32)
        den = pv[HEAD_DIM:HEAD_DIM + 1, :] + jnp.exp2(sink - m)
        out_t = pv[:HEAD_DIM, :] * (1.0 / den)
        qr = slice(j * BLOCK, (j + 1) * BLOCK)
        for t in range(2):
            pair = jnp.concatenate([out_t[:, (2 * t) * BLOCK:(2 * t + 1) * BLOCK],
                                    out_t[:, (2 * t + 1) * BLOCK:(2 * t + 2) * BLOCK]], axis=0)
            o_ref[qr, (2 * kvh + t) * LANES:(2 * kvh + t + 1) * LANES] = pair.T.astype(BF16)

    items = [(j, kvh) for j in range(q_blocks) for kvh in range(N_KV_HEADS)]
    pending = [scores(*item) for item in items[:AHEAD]]
    for i, item in enumerate(items):
        if i + AHEAD < len(items):
            pending.append(scores(*items[i + AHEAD]))
        softmax_pv(*item, pending.pop(0))


def _attention(qt, k, vt, sinks, bsz, seq, cast_weights, q_blocks=8):
    tq = q_blocks * BLOCK
    steps = seq // tq
    cast_in, cast_out, cast_shapes = _cast_specs(cast_weights, bsz * steps, lambda b, n: b * steps + n)
    nb = seq // BLOCK
    cur = lambda b, n: (b * steps + n, 0)
    prev = lambda b, n: (b * nb + jnp.maximum(n * q_blocks - 1, 0), 0)
    cur_t = lambda b, n: (0, b * steps + n)
    prev_t = lambda b, n: (0, b * nb + jnp.maximum(n * q_blocks - 1, 0))
    return pl.pallas_call(
        functools.partial(_attn_kernel, q_blocks=q_blocks, n_cast=len(cast_weights)),
        out_shape=[jax.ShapeDtypeStruct((bsz * seq, Q_DIM), BF16)] + cast_shapes,
        grid=(bsz, steps),
        in_specs=[pl.BlockSpec(memory_space=pltpu.SMEM),
                  pl.BlockSpec((Q_DIM, tq), cur_t),
                  pl.BlockSpec((BLOCK, KV_DIM), prev), pl.BlockSpec((tq, KV_DIM), cur),
                  pl.BlockSpec((KV_DIM, BLOCK), prev_t), pl.BlockSpec((KV_DIM, tq), cur_t)] + cast_in,
        out_specs=[pl.BlockSpec((tq, Q_DIM), cur)] + cast_out,
        scratch_shapes=[pltpu.VMEM((2, 2 * BLOCK, GROUP * BLOCK), F32)],
        compiler_params=_params("parallel", "parallel"),
        name="swa_attention",
    )(sinks, qt, k, k, vt, vt, *[w for w, _ in cast_weights])


def _out_mlp_kernel(a_ref, x_ref, wo_ref, g_ref, wu_ref, wd_ref, gn_ref, *out_refs, last):
    x1 = x_ref[...] + jnp.dot(a_ref[...], wo_ref[...], preferred_element_type=F32)
    h = _rmsnorm(x1, g_ref[...]).astype(BF16)
    x2 = x1
    for c in range(D_FF // FF_CHUNK):
        cols = slice(c * FF_CHUNK, (c + 1) * FF_CHUNK)
        u = jnp.maximum(jnp.dot(h, wu_ref[:, cols], preferred_element_type=F32), 0.0)
        x2 = x2 + jnp.dot((u * u).astype(BF16), wd_ref[cols, :], preferred_element_type=F32)
    normed = _rmsnorm(x2, gn_ref[...])
    if last:
        out_refs[0][...] = normed
    else:
        out_refs[0][...] = x2
        out_refs[1][...] = normed.astype(BF16)


def _out_mlp(a, x, w_o, gain, w_up, w_down, next_gain, last, tm=1024):
    n_tok = x.shape[0]
    row = pl.BlockSpec((tm, D_MODEL), lambda i: (i, 0))
    once = pl.Buffered(1)
    wspec = lambda shape: pl.BlockSpec(shape, lambda i: (0, 0), pipeline_mode=once)
    stream = jax.ShapeDtypeStruct((n_tok, D_MODEL), F32)
    return pl.pallas_call(
        functools.partial(_out_mlp_kernel, last=last),
        out_shape=[stream] if last else [stream, jax.ShapeDtypeStruct((n_tok, D_MODEL), BF16)],
        grid=(n_tok // tm,),
        in_specs=[row, row, wspec((D_MODEL, D_MODEL)), wspec((1, D_MODEL)),
                  wspec((D_MODEL, D_FF)), wspec((D_FF, D_MODEL)), wspec((1, D_MODEL))],
        out_specs=[row] if last else [row, row],
        compiler_params=_params("parallel"),
        name="out_proj_mlp",
    )(a, x, w_o, gain, w_up, w_down, next_gain)


def _hgrn_in_kernel(h_ref, w_ref, lbp_ref, q_ref, f_ref, v_ref, gate_ref):
    h = h_ref[...]
    lbp = lbp_ref[...]
    e = jnp.exp(lbp - jnp.max(lbp, axis=0, keepdims=True))
    sm = e / jnp.sum(e, axis=0, keepdims=True)
    lb = (sm[0:1] + sm[1:2]) - sm[0:1]

    def proj(c):
        return jnp.dot(h, w_ref[:, c * D_MODEL:(c + 1) * D_MODEL], preferred_element_type=F32)

    def by_head(ref, val):
        for hd in range(HGRN_HEADS):
            ref[hd] = val[:, hd * HGRN_DK:(hd + 1) * HGRN_DK]

    q = proj(0)
    f = proj(1)
    by_head(q_ref, (q * jax.nn.sigmoid(q)).astype(BF16))
    g = proj(3)
    by_head(f_ref, lb + (1.0 - lb) * jax.nn.sigmoid(f))
    i = proj(2)
    gate_ref[...] = (g * jax.nn.sigmoid(g)).astype(BF16)
    by_head(v_ref, i.astype(BF16))


def _hgrn_in(h, w_in, lb_params, tm=1024):
    n_tok = h.shape[0]
    head_major = pl.BlockSpec((HGRN_HEADS, tm, HGRN_DK), lambda i: (0, i, 0))
    once = pl.Buffered(1)
    return pl.pallas_call(
        _hgrn_in_kernel,
        out_shape=[jax.ShapeDtypeStruct((HGRN_HEADS, n_tok, HGRN_DK), BF16),
                   jax.ShapeDtypeStruct((HGRN_HEADS, n_tok, HGRN_DK), F32),
                   jax.ShapeDtypeStruct((HGRN_HEADS, n_tok, HGRN_DK), BF16),
                   jax.ShapeDtypeStruct((n_tok, D_MODEL), BF16)],
        grid=(n_tok // tm,),
        in_specs=[pl.BlockSpec((tm, D_MODEL), lambda i: (i, 0)),
                  pl.BlockSpec((D_MODEL, 4 * D_MODEL), lambda i: (0, 0), pipeline_mode=once),
                  _resident((2, D_MODEL))],
        out_specs=[head_major, head_major, head_major,
                   pl.BlockSpec((tm, D_MODEL), lambda i: (i, 0))],
        compiler_params=_params("parallel"),
        name="hgrn_in_proj",
    )(h, w_in, lb_params)


LEVELS = (32, 16, 8, 4, 2)
SPLIT = 2


def _decay_sum_matrix():
    t = np.arange(CHUNK)[:, None]
    u = np.arange(CHUNK)[None, :]
    mats = [u <= t]
    for m in LEVELS:
        blk = t // m
        odd = (blk % 2) == 1
        q_side = (u >= blk * m) & (u <= t)
        k_side = (u > t) & (u < (blk + 1) * m)
        mats.append(np.where(odd, q_side, k_side))
    d = np.concatenate(mats, axis=0).astype(np.float32)
    return np.concatenate([d] * SPLIT, axis=1)


def _level_index():
    t = np.arange(CHUNK)[:, None]
    s = np.arange(CHUNK)[None, :]
    x = t ^ s
    lvl = np.floor(np.log2(np.maximum(x, 1))).astype(np.int32)
    lvl = np.where(t == s, -1, lvl)
    return np.where(t < s, -2, lvl).astype(np.int32)


def _hgrn_scan_kernel(q_ref, f_ref, v_ref, gate_ref, gn_ref, dmat_ref, lvl_ref, a_ref, st_ref,
                      *, n_chunks):
    @pl.when(pl.program_id(1) == 0)
    def _():
        st_ref[...] = jnp.zeros_like(st_ref)

    pair_w = 2 * HGRN_DK
    row = lax.broadcasted_iota(jnp.int32, (CHUNK, pair_w), 0)
    lvl = lvl_ref[...]
    dmat = dmat_ref[...]
    zero_c = jnp.zeros((CHUNK, HGRN_DK), BF16)
    zero_s = jnp.zeros((HGRN_DK, HGRN_DK), BF16)

    def block_diag(a0, a1, zero):
        return jnp.concatenate([jnp.concatenate([a0, zero], axis=1),
                                jnp.concatenate([zero, a1], axis=1)], axis=0)

    pairs = range(HGRN_HEADS // 2)

    def gram(z):
        zb = block_diag(z[:, :HGRN_DK], z[:, HGRN_DK:], zero_c)
        return lax.dot_general(z, zb, NT_DIMS, preferred_element_type=F32)

    def load_and_sum(j):
        rows = slice(j * CHUNK, (j + 1) * CHUNK)
        out = []
        for p in pairs:
            cat = lambda ref: jnp.concatenate([ref[2 * p, rows, :], ref[2 * p + 1, rows, :]], axis=1)
            qt, fg, v = cat(q_ref).astype(F32), cat(f_ref), cat(v_ref)
            lf = jnp.log2(fg)
            pieces, rest = [], lf
            for _ in range(SPLIT):
                piece = rest.astype(BF16)
                pieces.append(piece)
                rest = rest - piece.astype(F32)
            sums = jnp.dot(dmat, jnp.concatenate(pieces, axis=0), preferred_element_type=F32)
            out.append((qt, fg, v, 1.0 - fg, sums))
        return out

    def scores_and_inter(staged):
        out = []
        for p in pairs:
            qt, fg, _, kk, sums = staged[p]
            qe = (qt * jnp.exp2(sums[0:CHUNK])).astype(BF16)
            st_bd = block_diag(st_ref[2 * p].astype(BF16), st_ref[2 * p + 1].astype(BF16), zero_s)
            o_inter = jnp.dot(qe, st_bd, preferred_element_type=F32)
            scores = jnp.zeros((CHUNK, 2 * CHUNK), F32)
            for li, m in enumerate(LEVELS):
                decay = jnp.exp2(sums[(1 + li) * CHUNK:(2 + li) * CHUNK])
                if m >= 8:
                    blocks = [slice(i * m, (i + 1) * m) for i in range(CHUNK // m)]
                    zq = jnp.concatenate([qt[b] * decay[b] for b in blocks[1::2]], axis=0)
                    zk = jnp.concatenate([kk[b] * decay[b] if i % 2 == 0 else jnp.zeros((m, pair_w), F32)
                                          for i, b in enumerate(blocks)], axis=0).astype(BF16)
                    part = lax.dot_general(zq.astype(BF16),
                                           block_diag(zk[:, :HGRN_DK], zk[:, HGRN_DK:], zero_c),
                                           NT_DIMS, preferred_element_type=F32)
                    zero_rows = jnp.zeros((m, 2 * CHUNK), F32)
                    g = jnp.concatenate([part[(i // 2) * m:(i // 2 + 1) * m] if i % 2 else zero_rows
                                         for i in range(CHUNK // m)], axis=0)
                else:
                    z = jnp.where(((row // m) % 2) == 1, qt, kk) * decay
                    g = gram(z.astype(BF16))
                scores = jnp.where(lvl == int(np.log2(m)), g, scores)
            z = jnp.where((row % 2) == 1, qt * fg, kk).astype(BF16)
            scores = jnp.where(lvl == 0, gram(z), scores)
            kkb = kk.astype(BF16)
            diag = lax.dot_general(qt.astype(BF16), block_diag(kkb[:, :HGRN_DK], kkb[:, HGRN_DK:], zero_c),
                                   NT_DIMS, preferred_element_type=F32)
            out.append((o_inter, jnp.where(lvl == -1, diag, scores)))
        return out

    def output_and_state(j, staged, scored):
        rows = slice(j * CHUNK, (j + 1) * CHUNK)
        outs = []
        for p in pairs:
            _, _, v, kk, sums = staged[p]
            o_inter, scores = scored[p]
            outs.append(o_inter + jnp.dot(scores.astype(BF16),
                                          block_diag(v[:, :HGRN_DK], v[:, HGRN_DK:], zero_c),
                                          preferred_element_type=F32))
            b = sums[0:CHUNK]
            b_last = b[CHUNK - 1:CHUNK, :]
            kd = (kk * jnp.exp2(b_last - b)).astype(BF16)
            dec = jnp.exp2(b_last)
            for hh, sl in ((2 * p, slice(0, HGRN_DK)), (2 * p + 1, slice(HGRN_DK, pair_w))):
                upd = lax.dot_general(kd[:, sl], v[:, sl], TN_DIMS, preferred_element_type=F32)
                dec_rows = jnp.broadcast_to(dec[:, sl], (HGRN_DK, HGRN_DK)).T
                st_ref[hh] = st_ref[hh] * dec_rows + upd
        o_all = jnp.concatenate(outs, axis=1)
        a_ref[rows, :] = (_rmsnorm(o_all, gn_ref[...]) * gate_ref[rows, :].astype(F32)).astype(BF16)

    staged = load_and_sum(0)
    for j in range(n_chunks):
        scored = scores_and_inter(staged)
        nxt = load_and_sum(j + 1) if j + 1 < n_chunks else None
        output_and_state(j, staged, scored)
        staged = nxt


def _hgrn_scan(q, f, v, gate, g_norm, bsz, seq, tc=1024):
    n_tok = q.shape[1]
    steps = seq // tc
    heads = pl.BlockSpec((HGRN_HEADS, tc, HGRN_DK), lambda b, c: (0, b * steps + c, 0))
    toks = pl.BlockSpec((tc, D_MODEL), lambda b, c: (b * steps + c, 0))
    dmat = jnp.asarray(_decay_sum_matrix(), BF16)
    lvl = jnp.asarray(np.tile(_level_index(), (1, 2)))
    return pl.pallas_call(
        functools.partial(_hgrn_scan_kernel, n_chunks=tc // CHUNK),
        out_shape=jax.ShapeDtypeStruct((n_tok, D_MODEL), BF16),
        grid=(bsz, steps),
        in_specs=[heads, heads, heads, toks, _resident((1, D_MODEL)), _resident(dmat.shape),
                  _resident(lvl.shape)],
        out_specs=toks,
        scratch_shapes=[pltpu.VMEM((HGRN_HEADS, HGRN_DK, HGRN_DK), F32)],
        compiler_params=_params("parallel", "arbitrary"),
        name="hgrn_scan",
    )(q, f, v, gate, g_norm, dmat, lvl)


def kernel(x, positions, mix_norm, mlp_norm, final_norm, attn_w_qkv, attn_b_qkv, attn_sinks,
           attn_w_o, hgrn_w_in, hgrn_g_norm, hgrn_w_o, hgrn_lower_bounds, mlp_w_up, mlp_w_down):
    bsz, seq, _ = x.shape
    n_tok = bsz * seq
    xf = x.reshape(n_tok, D_MODEL)
    gain = lambda g: g.reshape(1, D_MODEL).astype(F32)

    qt, k, vt = _qkv_proj(xf, positions, gain(mix_norm[0]), attn_w_qkv[0].astype(BF16),
                          attn_b_qkv[0].reshape(1, QKV_DIM).astype(F32))
    a, w_o0, w_up0, w_down0, w_in, w_o1, w_up1, w_down1 = _attention(
        qt, k, vt, attn_sinks[0].astype(F32), bsz, seq,
        [(attn_w_o, 0), (mlp_w_up, 0), (mlp_w_down, 0), (hgrn_w_in, 0), (hgrn_w_o, 0),
         (mlp_w_up, 1), (mlp_w_down, 1)])
    xf, h1 = _out_mlp(a, xf, w_o0, gain(mlp_norm[0]), w_up0, w_down0, gain(mix_norm[1]), last=False)

    hq, hf, hv, gate = _hgrn_in(h1, w_in, hgrn_lower_bounds.astype(F32))
    a = _hgrn_scan(hq, hf, hv, gate, gain(hgrn_g_norm[0]), bsz, seq)
    (out,) = _out_mlp(a, xf, w_o1, gain(mlp_norm[1]), w_up1, w_down1, gain(final_norm), last=True)
    return out.reshape(bsz, seq, D_MODEL)
```

```python
import functools

import numpy as np
import jax
import jax.numpy as jnp
from jax import lax
from jax.experimental import pallas as pl
from jax.experimental.pallas import tpu as pltpu

D_MODEL = 1024
HEAD_DIM = 64
N_Q_HEADS = 16
N_KV_HEADS = 4
GROUP = 4
Q_DIM = N_Q_HEADS * HEAD_DIM
KV_DIM = N_KV_HEADS * HEAD_DIM
QKV_DIM = Q_DIM + 2 * KV_DIM
WINDOW = 128
BLOCK = 128
ROT_DIM = 16
ROT_HALF = ROT_DIM // 2
ROPE_THETA = 500000.0
NEG_INF = -1e30
HGRN_HEADS = 8
HGRN_DK = 128
CHUNK = 64
D_FF = 4 * D_MODEL
FF_CHUNK = 1024
AHEAD = 2

ROW_TILE = 1024
SCAN_TILE = 512
ATTN_Q_BLOCKS = 8
NORM_EPS = 1e-5
LOG2E = 1.4426950408889634

LANES = 128
BF16_SUBLANES = 16
VMEM_LIMIT = 56 * 1024 * 1024

BF16 = jnp.bfloat16
F32 = jnp.float32

NT_DIMS = (((1,), (1,)), ((), ()))
TN_DIMS = (((0,), (0,)), ((), ()))


def _rmsnorm(x, gain):
    ms = jnp.mean(x * x, axis=-1, keepdims=True)
    return x * lax.rsqrt(ms + NORM_EPS) * gain


def _params(*sem):
    return pltpu.CompilerParams(dimension_semantics=sem, vmem_limit_bytes=VMEM_LIMIT)


def _resident(shape):
    return pl.BlockSpec(shape, lambda *_: (0,) * len(shape))


def _cast_specs(weights, n_steps, step_index):
    in_specs, out_specs, shapes = [], [], []
    for w, layer in weights:
        _, n_rows, n_cols = w.shape
        rows = n_rows // n_steps
        in_specs.append(pl.BlockSpec((None, rows, n_cols),
                                     lambda *ids, layer=layer: (layer, step_index(*ids), 0)))
        out_specs.append(pl.BlockSpec((rows, n_cols), lambda *ids: (step_index(*ids), 0)))
        shapes.append(jax.ShapeDtypeStruct((n_rows, n_cols), BF16))
    return in_specs, out_specs, shapes


def _cast_slabs(in_refs, out_refs):
    for src, dst in zip(in_refs, out_refs, strict=True):
        dst[...] = src[...].astype(BF16)


def _qkv_kernel(x_ref, pos_ref, invf_ref, g_ref, w_ref, b_ref, qt_ref, k_ref, vt_ref):
    h = _rmsnorm(x_ref[...], g_ref[...]).astype(BF16)
    qkv = jnp.dot(h, w_ref[...], preferred_element_type=F32) + b_ref[...]
    ang = pos_ref[...].astype(F32) * invf_ref[...]
    cos_t = jnp.cos(ang)
    sin_t = jnp.sin(ang)

    def rotate_t(tt):
        parts = []
        for hd in range(LANES // HEAD_DIM):
            base = hd * HEAD_DIM
            x1 = tt[base:base + ROT_HALF]
            x2 = tt[base + ROT_HALF:base + ROT_DIM]
            parts += [x1 * cos_t - x2 * sin_t, x2 * cos_t + x1 * sin_t,
                      tt[base + ROT_DIM:base + HEAD_DIM]]
        return jnp.concatenate(parts, axis=0)

    scale = HEAD_DIM ** -0.5 * LOG2E
    for j in range(Q_DIM // LANES):
        t = qkv[:, j * LANES:(j + 1) * LANES] * scale
        qt_ref[j * LANES:(j + 1) * LANES, :] = rotate_t(t.T).astype(BF16)
    for j in range(KV_DIM // LANES):
        t = qkv[:, Q_DIM + j * LANES:Q_DIM + (j + 1) * LANES]
        k_ref[:, j * LANES:(j + 1) * LANES] = rotate_t(t.T).T.astype(BF16)
        t = qkv[:, Q_DIM + KV_DIM + j * LANES:Q_DIM + KV_DIM + (j + 1) * LANES]
        vt_ref[j * LANES:(j + 1) * LANES, :] = t.T.astype(BF16)


def _qkv_proj(x, positions, gain, w_qkv, b_qkv):
    n_tok = x.shape[0]
    tm = ROW_TILE
    row = lambda w: pl.BlockSpec((tm, w), lambda i: (i, 0))
    col = lambda h: pl.BlockSpec((h, tm), lambda i: (0, i))
    inv_freq = ROPE_THETA ** (-jnp.arange(0, ROT_DIM, 2, dtype=F32) / ROT_DIM)
    return pl.pallas_call(
        _qkv_kernel,
        out_shape=[jax.ShapeDtypeStruct((Q_DIM, n_tok), BF16),
                   jax.ShapeDtypeStruct((n_tok, KV_DIM), BF16),
                   jax.ShapeDtypeStruct((KV_DIM, n_tok), BF16)],
        grid=(n_tok // tm,),
        in_specs=[row(D_MODEL), col(1), _resident((ROT_HALF, 1)), _resident((1, D_MODEL)),
                  _resident((D_MODEL, QKV_DIM)), _resident((1, QKV_DIM))],
        out_specs=[col(Q_DIM), row(KV_DIM), col(KV_DIM)],
        compiler_params=_params("parallel"),
        name="qkv_rope",
    )(x, positions.reshape(1, n_tok), inv_freq.reshape(ROT_HALF, 1), gain, w_qkv, b_qkv)


def _attn_kernel(sink_ref, qt_ref, kp_ref, kc_ref, vtp_ref, vtc_ref, *rest, q_blocks, n_cast):
    cast_in, o_ref, cast_out, bias_ref = rest[:n_cast], rest[n_cast], rest[n_cast + 1:-1], rest[-1]
    _cast_slabs(cast_in, cast_out)
    n = pl.program_id(1)
    band = 2 * BLOCK
    width = GROUP * BLOCK
    kj = lax.broadcasted_iota(jnp.int32, (band, width), 0)
    qi = lax.broadcasted_iota(jnp.int32, (band, width), 1) % BLOCK
    delta = qi + BLOCK - kj
    in_window = (delta >= 0) & (delta < WINDOW)
    bias_ref[0] = jnp.where(in_window & ((kj >= BLOCK) | (n > 0)), 0.0, NEG_INF)
    bias_ref[1] = jnp.where(in_window, 0.0, NEG_INF)
    head_of_lane = lax.broadcasted_iota(jnp.int32, (1, width), 1) // BLOCK
    zero_q = jnp.zeros((HEAD_DIM, width), BF16)
    ones_rows = jnp.ones((BF16_SUBLANES, band), BF16)

    def scores(j, kvh):
        tile = slice((kvh // 2) * LANES, (kvh // 2 + 1) * LANES)
        if j == 0:
            kband = jnp.concatenate([kp_ref[:, tile], kc_ref[:BLOCK, tile]], axis=0)
        else:
            kband = kc_ref[(j - 1) * BLOCK:(j + 1) * BLOCK, tile]
        qc = slice(j * BLOCK, (j + 1) * BLOCK)
        qt4 = jnp.concatenate(
            [qt_ref[(GROUP * kvh + g) * HEAD_DIM:(GROUP * kvh + g + 1) * HEAD_DIM, qc]
             for g in range(GROUP)], axis=1)
        rhs = jnp.concatenate([qt4, zero_q] if kvh % 2 == 0 else [zero_q, qt4], axis=0)
        s = jnp.dot(kband, rhs, preferred_element_type=F32)
        return s + bias_ref[0 if j == 0 else 1]

    def softmax_pv(j, kvh, s):
        hs = slice(kvh * HEAD_DIM, (kvh + 1) * HEAD_DIM)
        if j == 0:
            vt = jnp.concatenate([vtp_ref[hs, :], vtc_ref[hs, :BLOCK]], axis=1)
        else:
            vt = vtc_ref[hs, (j - 1) * BLOCK:(j + 1) * BLOCK]
        sink = jnp.zeros((1, width), F32)
        for g in range(GROUP):
            sink = jnp.where(head_of_lane == g, sink_ref[GROUP * kvh + g] * LOG2E, sink)
        m = jnp.maximum(jnp.max(s, axis=0, keepdims=True), sink)
        e = jnp.exp2(s - m).astype(BF16)
        pv = jnp.dot(jnp.concatenate([vt, ones_rows], axis=0), e,
                     preferred_element_type=F32)
        den = pv[HEAD_DIM:HEAD_DIM + 1, :] + jnp.exp2(sink - m)
        out_t = pv[:HEAD_DIM, :] * (1.0 / den)
        qr = slice(j * BLOCK, (j + 1) * BLOCK)
        for t in range(GROUP * HEAD_DIM // LANES):
            pair = jnp.concatenate([out_t[:, (2 * t) * BLOCK:(2 * t + 1) * BLOCK],
                                    out_t[:, (2 * t + 1) * BLOCK:(2 * t + 2) * BLOCK]], axis=0)
            o_ref[qr, (2 * kvh + t) * LANES:(2 * kvh + t + 1) * LANES] = pair.T.astype(BF16)

    items = [(j, kvh) for j in range(q_blocks) for kvh in range(N_KV_HEADS)]
    pending = [scores(*item) for item in items[:AHEAD]]
    for i, item in enumerate(items):
        if i + AHEAD < len(items):
            pending.append(scores(*items[i + AHEAD]))
        softmax_pv(*item, pending.pop(0))


def _attention(qt, k, vt, sinks, bsz, seq, cast_weights):
    q_blocks = ATTN_Q_BLOCKS
    tq = q_blocks * BLOCK
    steps = seq // tq
    cast_in, cast_out, cast_shapes = _cast_specs(cast_weights, bsz * steps, lambda b, n: b * steps + n)
    nb = seq // BLOCK
    cur = lambda b, n: (b * steps + n, 0)
    prev = lambda b, n: (b * nb + jnp.maximum(n * q_blocks - 1, 0), 0)
    cur_t = lambda b, n: (0, b * steps + n)
    prev_t = lambda b, n: (0, b * nb + jnp.maximum(n * q_blocks - 1, 0))
    return pl.pallas_call(
        functools.partial(_attn_kernel, q_blocks=q_blocks, n_cast=len(cast_weights)),
        out_shape=[jax.ShapeDtypeStruct((bsz * seq, Q_DIM), BF16)] + cast_shapes,
        grid=(bsz, steps),
        in_specs=[pl.BlockSpec(memory_space=pltpu.SMEM),
                  pl.BlockSpec((Q_DIM, tq), cur_t),
                  pl.BlockSpec((BLOCK, KV_DIM), prev), pl.BlockSpec((tq, KV_DIM), cur),
                  pl.BlockSpec((KV_DIM, BLOCK), prev_t), pl.BlockSpec((KV_DIM, tq), cur_t)] + cast_in,
        out_specs=[pl.BlockSpec((tq, Q_DIM), cur)] + cast_out,
        scratch_shapes=[pltpu.VMEM((2, 2 * BLOCK, GROUP * BLOCK), F32)],
        compiler_params=_params("parallel", "parallel"),
        name="swa_attention",
    )(sinks, qt, k, k, vt, vt, *[w for w, _ in cast_weights])


def _out_mlp_kernel(a_ref, x_ref, wo_ref, g_ref, wu_ref, wd_ref, gn_ref, *out_refs, last):
    x1 = x_ref[...] + jnp.dot(a_ref[...], wo_ref[...], preferred_element_type=F32)
    h = _rmsnorm(x1, g_ref[...]).astype(BF16)
    x2 = x1
    for c in range(D_FF // FF_CHUNK):
        cols = slice(c * FF_CHUNK, (c + 1) * FF_CHUNK)
        u = jnp.maximum(jnp.dot(h, wu_ref[:, cols], preferred_element_type=F32), 0.0)
        x2 = x2 + jnp.dot((u * u).astype(BF16), wd_ref[cols, :], preferred_element_type=F32)
    normed = _rmsnorm(x2, gn_ref[...])
    if last:
        out_refs[0][...] = normed
    else:
        out_refs[0][...] = x2
        out_refs[1][...] = normed.astype(BF16)


def _out_mlp(a, x, w_o, gain, w_up, w_down, next_gain, last):
    n_tok = x.shape[0]
    tm = ROW_TILE
    row = pl.BlockSpec((tm, D_MODEL), lambda i: (i, 0))
    once = pl.Buffered(1)
    wspec = lambda shape: pl.BlockSpec(shape, lambda i: (0, 0), pipeline_mode=once)
    stream = jax.ShapeDtypeStruct((n_tok, D_MODEL), F32)
    return pl.pallas_call(
        functools.partial(_out_mlp_kernel, last=last),
        out_shape=[stream] if last else [stream, jax.ShapeDtypeStruct((n_tok, D_MODEL), BF16)],
        grid=(n_tok // tm,),
        in_specs=[row, row, wspec((D_MODEL, D_MODEL)), wspec((1, D_MODEL)),
                  wspec((D_MODEL, D_FF)), wspec((D_FF, D_MODEL)), wspec((1, D_MODEL))],
        out_specs=[row] if last else [row, row],
        compiler_params=_params("parallel"),
        name="out_proj_mlp",
    )(a, x, w_o, gain, w_up, w_down, next_gain)


def _hgrn_in_kernel(h_ref, w_ref, lbp_ref, q_ref, f_ref, v_ref, gate_ref):
    h = h_ref[...]
    lbp = lbp_ref[...]
    e = jnp.exp(lbp - jnp.max(lbp, axis=0, keepdims=True))
    sm = e / jnp.sum(e, axis=0, keepdims=True)
    lb = (sm[0:1] + sm[1:2]) - sm[0:1]

    def proj(c):
        return jnp.dot(h, w_ref[:, c * D_MODEL:(c + 1) * D_MODEL], preferred_element_type=F32)

    q = proj(0)
    f = proj(1)
    q_ref[...] = (q * jax.nn.sigmoid(q)).astype(BF16)
    g = proj(3)
    f_ref[...] = lb + (1.0 - lb) * jax.nn.sigmoid(f)
    i = proj(2)
    gate_ref[...] = (g * jax.nn.sigmoid(g)).astype(BF16)
    v_ref[...] = i.astype(BF16)


def _hgrn_in(h, w_in, lb_params):
    n_tok = h.shape[0]
    tm = ROW_TILE
    row = pl.BlockSpec((tm, D_MODEL), lambda i: (i, 0))
    once = pl.Buffered(1)
    out = lambda dtype: jax.ShapeDtypeStruct((n_tok, D_MODEL), dtype)
    return pl.pallas_call(
        _hgrn_in_kernel,
        out_shape=[out(BF16), out(F32), out(BF16), out(BF16)],
        grid=(n_tok // tm,),
        in_specs=[row,
                  pl.BlockSpec((D_MODEL, 4 * D_MODEL), lambda i: (0, 0), pipeline_mode=once),
                  _resident((2, D_MODEL))],
        out_specs=[row, row, row, row],
        compiler_params=_params("parallel"),
        name="hgrn_in_proj",
    )(h, w_in, lb_params)


LEVELS = (32, 16, 8, 4, 2)
SPLIT = 2


def _decay_sum_matrix():
    t = np.arange(CHUNK)[:, None]
    u = np.arange(CHUNK)[None, :]
    mats = [u <= t]
    for m in LEVELS:
        blk = t // m
        odd = (blk % 2) == 1
        q_side = (u >= blk * m) & (u <= t)
        k_side = (u > t) & (u < (blk + 1) * m)
        mats.append(np.where(odd, q_side, k_side))
    d = np.concatenate(mats, axis=0).astype(np.float32)
    return np.concatenate([d] * SPLIT, axis=1)


def _level_index():
    t = np.arange(CHUNK)[:, None]
    s = np.arange(CHUNK)[None, :]
    x = t ^ s
    lvl = np.floor(np.log2(np.maximum(x, 1))).astype(np.int32)
    lvl = np.where(t == s, -1, lvl)
    return np.where(t < s, -2, lvl).astype(np.int32)


def _hgrn_scan_kernel(q_ref, f_ref, v_ref, gate_ref, gn_ref, dmat_ref, lvl_ref, a_ref, st_ref,
                      *, n_chunks):
    @pl.when(pl.program_id(1) == 0)
    def _():
        st_ref[...] = jnp.zeros_like(st_ref)

    pair_w = 2 * HGRN_DK
    row = lax.broadcasted_iota(jnp.int32, (CHUNK, pair_w), 0)
    lvl = lvl_ref[...]
    dmat = dmat_ref[...]
    zero_c = jnp.zeros((CHUNK, HGRN_DK), BF16)
    zero_s = jnp.zeros((HGRN_DK, HGRN_DK), BF16)

    def block_diag(a0, a1, zero):
        return jnp.concatenate([jnp.concatenate([a0, zero], axis=1),
                                jnp.concatenate([zero, a1], axis=1)], axis=0)

    pairs = range(HGRN_HEADS // 2)

    def gram(z):
        zb = block_diag(z[:, :HGRN_DK], z[:, HGRN_DK:], zero_c)
        return lax.dot_general(z, zb, NT_DIMS, preferred_element_type=F32)

    def load_and_sum(j):
        rows = slice(j * CHUNK, (j + 1) * CHUNK)
        out = []
        for p in pairs:
            cols = slice(p * pair_w, (p + 1) * pair_w)
            qt, fg, v = q_ref[rows, cols].astype(F32), f_ref[rows, cols], v_ref[rows, cols]
            lf = jnp.log2(fg)
            pieces, rest = [], lf
            for _ in range(SPLIT):
                piece = rest.astype(BF16)
                pieces.append(piece)
                rest = rest - piece.astype(F32)
            sums = jnp.dot(dmat, jnp.concatenate(pieces, axis=0), preferred_element_type=F32)
            out.append((qt, fg, v, 1.0 - fg, sums))
        return out

    def scores_and_inter(staged):
        out = []
        for p in pairs:
            qt, fg, _, kk, sums = staged[p]
            qe = (qt * jnp.exp2(sums[0:CHUNK])).astype(BF16)
            st_bd = block_diag(st_ref[2 * p].astype(BF16), st_ref[2 * p + 1].astype(BF16), zero_s)
            o_inter = jnp.dot(qe, st_bd, preferred_element_type=F32)
            scores = jnp.zeros((CHUNK, 2 * CHUNK), F32)
            for li, m in enumerate(LEVELS):
                decay = jnp.exp2(sums[(1 + li) * CHUNK:(2 + li) * CHUNK])
                if m >= 8:
                    blocks = [slice(i * m, (i + 1) * m) for i in range(CHUNK // m)]
                    zq = jnp.concatenate([qt[b] * decay[b] for b in blocks[1::2]], axis=0)
                    zk = jnp.concatenate([kk[b] * decay[b] if i % 2 == 0 else jnp.zeros((m, pair_w), F32)
                                          for i, b in enumerate(blocks)], axis=0).astype(BF16)
                    part = lax.dot_general(zq.astype(BF16),
                                           block_diag(zk[:, :HGRN_DK], zk[:, HGRN_DK:], zero_c),
                                           NT_DIMS, preferred_element_type=F32)
                    zero_rows = jnp.zeros((m, 2 * CHUNK), F32)
                    g = jnp.concatenate([part[(i // 2) * m:(i // 2 + 1) * m] if i % 2 else zero_rows
                                         for i in range(CHUNK // m)], axis=0)
                else:
                    z = jnp.where(((row // m) % 2) == 1, qt, kk) * decay
                    g = gram(z.astype(BF16))
                scores = jnp.where(lvl == int(np.log2(m)), g, scores)
            z = jnp.where((row % 2) == 1, qt * fg, kk).astype(BF16)
            scores = jnp.where(lvl == 0, gram(z), scores)
            kkb = kk.astype(BF16)
            diag = lax.dot_general(qt.astype(BF16), block_diag(kkb[:, :HGRN_DK], kkb[:, HGRN_DK:], zero_c),
                                   NT_DIMS, preferred_element_type=F32)
            out.append((o_inter, jnp.where(lvl == -1, diag, scores)))
        return out

    def output_and_state(j, staged, scored):
        rows = slice(j * CHUNK, (j + 1) * CHUNK)
        outs = []
        for p in pairs:
            _, _, v, kk, sums = staged[p]
            o_inter, scores = scored[p]
            outs.append(o_inter + jnp.dot(scores.astype(BF16),
                                          block_diag(v[:, :HGRN_DK], v[:, HGRN_DK:], zero_c),
                                          preferred_element_type=F32))
            b = sums[0:CHUNK]
            b_last = b[CHUNK - 1:CHUNK, :]
            kd = (kk * jnp.exp2(b_last - b)).astype(BF16)
            dec = jnp.exp2(b_last)
            for hh, sl in ((2 * p, slice(0, HGRN_DK)), (2 * p + 1, slice(HGRN_DK, pair_w))):
                upd = lax.dot_general(kd[:, sl], v[:, sl], TN_DIMS, preferred_element_type=F32)
                dec_rows = jnp.broadcast_to(dec[:, sl], (HGRN_DK, HGRN_DK)).T
                st_ref[hh] = st_ref[hh] * dec_rows + upd
        o_all = jnp.concatenate(outs, axis=1)
        a_ref[rows, :] = (_rmsnorm(o_all, gn_ref[...]) * gate_ref[rows, :].astype(F32)).astype(BF16)

    staged = load_and_sum(0)
    for j in range(n_chunks):
        scored = scores_and_inter(staged)
        nxt = load_and_sum(j + 1) if j + 1 < n_chunks else None
        output_and_state(j, staged, scored)
        staged = nxt


def _hgrn_scan(q, f, v, gate, g_norm, bsz, seq):
    n_tok = q.shape[0]
    tc = SCAN_TILE
    steps = seq // tc
    toks = pl.BlockSpec((tc, D_MODEL), lambda b, c: (b * steps + c, 0))
    dmat = jnp.asarray(_decay_sum_matrix(), BF16)
    lvl = jnp.asarray(np.tile(_level_index(), (1, 2)))
    return pl.pallas_call(
        functools.partial(_hgrn_scan_kernel, n_chunks=tc // CHUNK),
        out_shape=jax.ShapeDtypeStruct((n_tok, D_MODEL), BF16),
        grid=(bsz, steps),
        in_specs=[toks, toks, toks, toks, _resident((1, D_MODEL)), _resident(dmat.shape),
                  _resident(lvl.shape)],
        out_specs=toks,
        scratch_shapes=[pltpu.VMEM((HGRN_HEADS, HGRN_DK, HGRN_DK), F32)],
        compiler_params=_params("parallel", "arbitrary"),
        name="hgrn_scan",
    )(q, f, v, gate, g_norm, dmat, lvl)


def kernel(x, positions, mix_norm, mlp_norm, final_norm, attn_w_qkv, attn_b_qkv, attn_sinks,
           attn_w_o, hgrn_w_in, hgrn_g_norm, hgrn_w_o, hgrn_lower_bounds, mlp_w_up, mlp_w_down):
    bsz, seq, _ = x.shape
    n_tok = bsz * seq
    xf = x.reshape(n_tok, D_MODEL)
    gain = lambda g: g.reshape(1, D_MODEL).astype(F32)

    qt, k, vt = _qkv_proj(xf, positions, gain(mix_norm[0]), attn_w_qkv[0].astype(BF16),
                          attn_b_qkv[0].reshape(1, QKV_DIM).astype(F32))
    a, w_o0, w_up0, w_down0, w_in, w_o1, w_up1, w_down1 = _attention(
        qt, k, vt, attn_sinks[0].astype(F32), bsz, seq,
        [(attn_w_o, 0), (mlp_w_up, 0), (mlp_w_down, 0), (hgrn_w_in, 0), (hgrn_w_o, 0),
         (mlp_w_up, 1), (mlp_w_down, 1)])
    xf, h1 = _out_mlp(a, xf, w_o0, gain(mlp_norm[0]), w_up0, w_down0, gain(mix_norm[1]), last=False)

    hq, hf, hv, gate = _hgrn_in(h1, w_in, hgrn_lower_bounds.astype(F32))
    a = _hgrn_scan(hq, hf, hv, gate, gain(hgrn_g_norm[0]), bsz, seq)
    (out,) = _out_mlp(a, xf, w_o1, gain(mlp_norm[1]), w_up1, w_down1, gain(final_norm), last=True)
    return out.reshape(bsz, seq, D_MODEL)
```

```python
import functools

import numpy as np
import jax
import jax.numpy as jnp
from jax import lax
from jax.experimental import pallas as pl
from jax.experimental.pallas import tpu as pltpu

D_MODEL = 1024
HEAD_DIM = 64
N_Q_HEADS = 16
N_KV_HEADS = 4
GROUP = 4
Q_DIM = N_Q_HEADS * HEAD_DIM
KV_DIM = N_KV_HEADS * HEAD_DIM
QKV_DIM = Q_DIM + 2 * KV_DIM
WINDOW = 128
BLOCK = 128
ROT_DIM = 16
ROT_HALF = ROT_DIM // 2
ROPE_THETA = 500000.0
NEG_INF = -1e30
HGRN_HEADS = 8
HGRN_DK = 128
CHUNK = 64
D_FF = 4 * D_MODEL
FF_CHUNK = 1024
AHEAD = 2

ROW_TILE = 1024
SCAN_TILE = 512
ATTN_Q_BLOCKS = 8
NORM_EPS = 1e-5
LOG2E = 1.4426950408889634

LANES = 128
BF16_SUBLANES = 16
VMEM_LIMIT = 56 * 1024 * 1024

BF16 = jnp.bfloat16
F32 = jnp.float32

NT_DIMS = (((1,), (1,)), ((), ()))
TN_DIMS = (((0,), (0,)), ((), ()))


def _rmsnorm(x, gain):
    ms = jnp.mean(x * x, axis=-1, keepdims=True)
    return x * lax.rsqrt(ms + NORM_EPS) * gain


def _params(*sem):
    return pltpu.CompilerParams(dimension_semantics=sem, vmem_limit_bytes=VMEM_LIMIT)


def _resident(shape):
    return pl.BlockSpec(shape, lambda *_: (0,) * len(shape))


def _cast_specs(weights, n_steps, step_index):
    in_specs, out_specs, shapes = [], [], []
    for w, layer in weights:
        _, n_rows, n_cols = w.shape
        rows = n_rows // n_steps
        in_specs.append(pl.BlockSpec((None, rows, n_cols),
                                     lambda *ids, layer=layer: (layer, step_index(*ids), 0)))
        out_specs.append(pl.BlockSpec((rows, n_cols), lambda *ids: (step_index(*ids), 0)))
        shapes.append(jax.ShapeDtypeStruct((n_rows, n_cols), BF16))
    return in_specs, out_specs, shapes


def _cast_slabs(in_refs, out_refs):
    for src, dst in zip(in_refs, out_refs, strict=True):
        dst[...] = src[...].astype(BF16)


def _qkv_kernel(x_ref, pos_ref, invf_ref, g_ref, w_ref, b_ref, qt_ref, k_ref, vt_ref):
    h = _rmsnorm(x_ref[...], g_ref[...]).astype(BF16)
    qkv = jnp.dot(h, w_ref[...], preferred_element_type=F32) + b_ref[...]
    ang = pos_ref[...].astype(F32) * invf_ref[...]
    cos_t = jnp.cos(ang)
    sin_t = jnp.sin(ang)

    def rotate_t(tt):
        parts = []
        for hd in range(LANES // HEAD_DIM):
            base = hd * HEAD_DIM
            x1 = tt[base:base + ROT_HALF]
            x2 = tt[base + ROT_HALF:base + ROT_DIM]
            parts += [x1 * cos_t - x2 * sin_t, x2 * cos_t + x1 * sin_t,
                      tt[base + ROT_DIM:base + HEAD_DIM]]
        return jnp.concatenate(parts, axis=0)

    scale = HEAD_DIM ** -0.5 * LOG2E
    for j in range(Q_DIM // LANES):
        t = qkv[:, j * LANES:(j + 1) * LANES] * scale
        qt_ref[j * LANES:(j + 1) * LANES, :] = rotate_t(t.T).astype(BF16)
    for j in range(KV_DIM // LANES):
        t = qkv[:, Q_DIM + j * LANES:Q_DIM + (j + 1) * LANES]
        k_ref[:, j * LANES:(j + 1) * LANES] = rotate_t(t.T).T.astype(BF16)
        t = qkv[:, Q_DIM + KV_DIM + j * LANES:Q_DIM + KV_DIM + (j + 1) * LANES]
        vt_ref[j * LANES:(j + 1) * LANES, :] = t.T.astype(BF16)


def _qkv_proj(x, positions, gain, w_qkv, b_qkv):
    n_tok = x.shape[0]
    tm = ROW_TILE
    row = lambda w: pl.BlockSpec((tm, w), lambda i: (i, 0))
    col = lambda h: pl.BlockSpec((h, tm), lambda i: (0, i))
    inv_freq = ROPE_THETA ** (-jnp.arange(0, ROT_DIM, 2, dtype=F32) / ROT_DIM)
    return pl.pallas_call(
        _qkv_kernel,
        out_shape=[jax.ShapeDtypeStruct((Q_DIM, n_tok), BF16),
                   jax.ShapeDtypeStruct((n_tok, KV_DIM), BF16),
                   jax.ShapeDtypeStruct((KV_DIM, n_tok), BF16)],
        grid=(n_tok // tm,),
        in_specs=[row(D_MODEL), col(1), _resident((ROT_HALF, 1)), _resident((1, D_MODEL)),
                  _resident((D_MODEL, QKV_DIM)), _resident((1, QKV_DIM))],
        out_specs=[col(Q_DIM), row(KV_DIM), col(KV_DIM)],
        compiler_params=_params("parallel"),
        name="qkv_rope",
    )(x, positions.reshape(1, n_tok), inv_freq.reshape(ROT_HALF, 1), gain, w_qkv, b_qkv)


def _attn_kernel(sink_ref, qt_ref, kp_ref, kc_ref, vtp_ref, vtc_ref, *rest, q_blocks, n_cast):
    cast_in, o_ref, cast_out, bias_ref = rest[:n_cast], rest[n_cast], rest[n_cast + 1:-1], rest[-1]
    _cast_slabs(cast_in, cast_out)
    n = pl.program_id(1)
    band = 2 * BLOCK
    width = GROUP * BLOCK
    kj = lax.broadcasted_iota(jnp.int32, (band, width), 0)
    qi = lax.broadcasted_iota(jnp.int32, (band, width), 1) % BLOCK
    delta = qi + BLOCK - kj
    in_window = (delta >= 0) & (delta < WINDOW)
    bias_ref[0] = jnp.where(in_window & ((kj >= BLOCK) | (n > 0)), 0.0, NEG_INF)
    bias_ref[1] = jnp.where(in_window, 0.0, NEG_INF)
    head_of_lane = lax.broadcasted_iota(jnp.int32, (1, width), 1) // BLOCK
    zero_q = jnp.zeros((HEAD_DIM, width), BF16)
    ones_rows = jnp.ones((BF16_SUBLANES, band), BF16)

    def scores(j, kvh):
        tile = slice((kvh // 2) * LANES, (kvh // 2 + 1) * LANES)
        if j == 0:
            kband = jnp.concatenate([kp_ref[:, tile], kc_ref[:BLOCK, tile]], axis=0)
        else:
            kband = kc_ref[(j - 1) * BLOCK:(j + 1) * BLOCK, tile]
        qc = slice(j * BLOCK, (j + 1) * BLOCK)
        qt4 = jnp.concatenate(
            [qt_ref[(GROUP * kvh + g) * HEAD_DIM:(GROUP * kvh + g + 1) * HEAD_DIM, qc]
             for g in range(GROUP)], axis=1)
        rhs = jnp.concatenate([qt4, zero_q] if kvh % 2 == 0 else [zero_q, qt4], axis=0)
        s = jnp.dot(kband, rhs, preferred_element_type=F32)
        return s + bias_ref[0 if j == 0 else 1]

    def softmax_pv(j, kvh, s):
        hs = slice(kvh * HEAD_DIM, (kvh + 1) * HEAD_DIM)
        if j == 0:
            vt = jnp.concatenate([vtp_ref[hs, :], vtc_ref[hs, :BLOCK]], axis=1)
        else:
            vt = vtc_ref[hs, (j - 1) * BLOCK:(j + 1) * BLOCK]
        sink = jnp.zeros((1, width), F32)
        for g in range(GROUP):
            sink = jnp.where(head_of_lane == g, sink_ref[GROUP * kvh + g] * LOG2E, sink)
        m = jnp.maximum(jnp.max(s, axis=0, keepdims=True), sink)
        e = jnp.exp2(s - m).astype(BF16)
        pv = jnp.dot(jnp.concatenate([vt, ones_rows], axis=0), e,
                     preferred_element_type=F32)
        den = pv[HEAD_DIM:HEAD_DIM + 1, :] + jnp.exp2(sink - m)
        out_t = pv[:HEAD_DIM, :] * (1.0 / den)
        qr = slice(j * BLOCK, (j + 1) * BLOCK)
        for t in range(GROUP * HEAD_DIM // LANES):
            pair = jnp.concatenate([out_t[:, (2 * t) * BLOCK:(2 * t + 1) * BLOCK],
                                    out_t[:, (2 * t + 1) * BLOCK:(2 * t + 2) * BLOCK]], axis=0)
            o_ref[qr, (2 * kvh + t) * LANES:(2 * kvh + t + 1) * LANES] = pair.T.astype(BF16)

    items = [(j, kvh) for j in range(q_blocks) for kvh in range(N_KV_HEADS)]
    pending = [scores(*item) for item in items[:AHEAD]]
    for i, item in enumerate(items):
        if i + AHEAD < len(items):
            pending.append(scores(*items[i + AHEAD]))
        softmax_pv(*item, pending.pop(0))


def _attention(qt, k, vt, sinks, bsz, seq, cast_weights):
    q_blocks = ATTN_Q_BLOCKS
    tq = q_blocks * BLOCK
    steps = seq // tq
    cast_in, cast_out, cast_shapes = _cast_specs(cast_weights, bsz * steps, lambda b, n: b * steps + n)
    nb = seq // BLOCK
    cur = lambda b, n: (b * steps + n, 0)
    prev = lambda b, n: (b * nb + jnp.maximum(n * q_blocks - 1, 0), 0)
    cur_t = lambda b, n: (0, b * steps + n)
    prev_t = lambda b, n: (0, b * nb + jnp.maximum(n * q_blocks - 1, 0))
    return pl.pallas_call(
        functools.partial(_attn_kernel, q_blocks=q_blocks, n_cast=len(cast_weights)),
        out_shape=[jax.ShapeDtypeStruct((bsz * seq, Q_DIM), BF16)] + cast_shapes,
        grid=(bsz, steps),
        in_specs=[pl.BlockSpec(memory_space=pltpu.SMEM),
                  pl.BlockSpec((Q_DIM, tq), cur_t),
                  pl.BlockSpec((BLOCK, KV_DIM), prev), pl.BlockSpec((tq, KV_DIM), cur),
                  pl.BlockSpec((KV_DIM, BLOCK), prev_t), pl.BlockSpec((KV_DIM, tq), cur_t)] + cast_in,
        out_specs=[pl.BlockSpec((tq, Q_DIM), cur)] + cast_out,
        scratch_shapes=[pltpu.VMEM((2, 2 * BLOCK, GROUP * BLOCK), F32)],
        compiler_params=_params("parallel", "parallel"),
        name="swa_attention",
    )(sinks, qt, k, k, vt, vt, *[w for w, _ in cast_weights])


def _out_mlp_kernel(a_ref, x_ref, wo_ref, g_ref, wu_ref, wd_ref, gn_ref, *out_refs, last):
    x1 = x_ref[...] + jnp.dot(a_ref[...], wo_ref[...], preferred_element_type=F32)
    h = _rmsnorm(x1, g_ref[...]).astype(BF16)
    x2 = x1
    for c in range(D_FF // FF_CHUNK):
        cols = slice(c * FF_CHUNK, (c + 1) * FF_CHUNK)
        u = jnp.maximum(jnp.dot(h, wu_ref[:, cols], preferred_element_type=F32), 0.0)
        x2 = x2 + jnp.dot((u * u).astype(BF16), wd_ref[cols, :], preferred_element_type=F32)
    normed = _rmsnorm(x2, gn_ref[...])
    if last:
        out_refs[0][...] = normed
    else:
        out_refs[0][...] = x2
        out_refs[1][...] = normed.astype(BF16)


def _out_mlp(a, x, w_o, gain, w_up, w_down, next_gain, last):
    n_tok = x.shape[0]
    tm = ROW_TILE
    row = pl.BlockSpec((tm, D_MODEL), lambda i: (i, 0))
    once = pl.Buffered(1)
    wspec = lambda shape: pl.BlockSpec(shape, lambda i: (0, 0), pipeline_mode=once)
    stream = jax.ShapeDtypeStruct((n_tok, D_MODEL), F32)
    return pl.pallas_call(
        functools.partial(_out_mlp_kernel, last=last),
        out_shape=[stream] if last else [stream, jax.ShapeDtypeStruct((n_tok, D_MODEL), BF16)],
        grid=(n_tok // tm,),
        in_specs=[row, row, wspec((D_MODEL, D_MODEL)), wspec((1, D_MODEL)),
                  wspec((D_MODEL, D_FF)), wspec((D_FF, D_MODEL)), wspec((1, D_MODEL))],
        out_specs=[row] if last else [row, row],
        compiler_params=_params("parallel"),
        name="out_proj_mlp",
    )(a, x, w_o, gain, w_up, w_down, next_gain)


def _hgrn_in_kernel(h_ref, w_ref, lbp_ref, q_ref, f_ref, v_ref, gate_ref):
    h = h_ref[...]
    lbp = lbp_ref[...]
    e = jnp.exp(lbp - jnp.max(lbp, axis=0, keepdims=True))
    sm = e / jnp.sum(e, axis=0, keepdims=True)
    lb = (sm[0:1] + sm[1:2]) - sm[0:1]

    def proj(c):
        return jnp.dot(h, w_ref[:, c * D_MODEL:(c + 1) * D_MODEL], preferred_element_type=F32)

    q = proj(0)
    f = proj(1)
    q_ref[...] = (q * jax.nn.sigmoid(q)).astype(BF16)
    g = proj(3)
    f_ref[...] = lb + (1.0 - lb) * jax.nn.sigmoid(f)
    i = proj(2)
    gate_ref[...] = (g * jax.nn.sigmoid(g)).astype(BF16)
    v_ref[...] = i.astype(BF16)


def _hgrn_in(h, w_in, lb_params):
    n_tok = h.shape[0]
    tm = ROW_TILE
    row = pl.BlockSpec((tm, D_MODEL), lambda i: (i, 0))
    once = pl.Buffered(1)
    out = lambda dtype: jax.ShapeDtypeStruct((n_tok, D_MODEL), dtype)
    return pl.pallas_call(
        _hgrn_in_kernel,
        out_shape=[out(BF16), out(F32), out(BF16), out(BF16)],
        grid=(n_tok // tm,),
        in_specs=[row,
                  pl.BlockSpec((D_MODEL, 4 * D_MODEL), lambda i: (0, 0), pipeline_mode=once),
                  _resident((2, D_MODEL))],
        out_specs=[row, row, row, row],
        compiler_params=_params("parallel"),
        name="hgrn_in_proj",
    )(h, w_in, lb_params)


LEVELS = (32, 16, 8, 4, 2)
SPLIT = 2


def _decay_sum_matrix():
    t = np.arange(CHUNK)[:, None]
    u = np.arange(CHUNK)[None, :]
    mats = [u <= t]
    for m in LEVELS:
        blk = t // m
        odd = (blk % 2) == 1
        q_side = (u >= blk * m) & (u <= t)
        k_side = (u > t) & (u < (blk + 1) * m)
        mats.append(np.where(odd, q_side, k_side))
    d = np.concatenate(mats, axis=0).astype(np.float32)
    return np.concatenate([d] * SPLIT, axis=1)


def _level_index():
    t = np.arange(CHUNK)[:, None]
    s = np.arange(CHUNK)[None, :]
    x = t ^ s
    lvl = np.floor(np.log2(np.maximum(x, 1))).astype(np.int32)
    lvl = np.where(t == s, -1, lvl)
    return np.where(t < s, -2, lvl).astype(np.int32)


def _hgrn_scan_kernel(q_ref, f_ref, v_ref, gate_ref, gn_ref, dmat_ref, lvl_ref, a_ref, st_ref,
                      *, n_chunks):
    @pl.when(pl.program_id(1) == 0)
    def _():
        st_ref[...] = jnp.zeros_like(st_ref)

    pair_w = 2 * HGRN_DK
    row = lax.broadcasted_iota(jnp.int32, (CHUNK, pair_w), 0)
    lvl = lvl_ref[...]
    head0_cols = lax.broadcasted_iota(jnp.int32, (CHUNK, 2 * CHUNK), 1) < CHUNK
    dmat = dmat_ref[...]
    zero_c = jnp.zeros((CHUNK, HGRN_DK), BF16)
    zero_s = jnp.zeros((HGRN_DK, HGRN_DK), BF16)

    def block_diag(a0, a1, zero):
        return jnp.concatenate([jnp.concatenate([a0, zero], axis=1),
                                jnp.concatenate([zero, a1], axis=1)], axis=0)

    pairs = range(HGRN_HEADS // 2)

    def gram(z):
        zb = block_diag(z[:, :HGRN_DK], z[:, HGRN_DK:], zero_c)
        return lax.dot_general(z, zb, NT_DIMS, preferred_element_type=F32)

    def load_and_sum(j):
        rows = slice(j * CHUNK, (j + 1) * CHUNK)
        out = []
        for p in pairs:
            cols = slice(p * pair_w, (p + 1) * pair_w)
            qt, fg, v = q_ref[rows, cols].astype(F32), f_ref[rows, cols], v_ref[rows, cols]
            lf = jnp.log2(fg)
            pieces, rest = [], lf
            for _ in range(SPLIT):
                piece = rest.astype(BF16)
                pieces.append(piece)
                rest = rest - piece.astype(F32)
            sums = jnp.dot(dmat, jnp.concatenate(pieces, axis=0), preferred_element_type=F32)
            out.append((qt, fg, v, 1.0 - fg, sums))
        return out

    def scores_and_inter(staged):
        out = []
        for p in pairs:
            qt, fg, _, kk, sums = staged[p]
            qe = (qt * jnp.exp2(sums[0:CHUNK])).astype(BF16)
            st_bd = block_diag(st_ref[2 * p].astype(BF16), st_ref[2 * p + 1].astype(BF16), zero_s)
            o_inter = jnp.dot(qe, st_bd, preferred_element_type=F32)
            scores = jnp.zeros((CHUNK, 2 * CHUNK), F32)
            for li, m in enumerate(LEVELS):
                decay = jnp.exp2(sums[(1 + li) * CHUNK:(2 + li) * CHUNK])
                if m >= 8:
                    blocks = [slice(i * m, (i + 1) * m) for i in range(CHUNK // m)]
                    zq = jnp.concatenate([qt[b] * decay[b] for b in blocks[1::2]], axis=0)
                    zk = jnp.concatenate([kk[b] * decay[b] if i % 2 == 0 else jnp.zeros((m, pair_w), F32)
                                          for i, b in enumerate(blocks)], axis=0).astype(BF16)
                    part = lax.dot_general(zq.astype(BF16),
                                           block_diag(zk[:, :HGRN_DK], zk[:, HGRN_DK:], zero_c),
                                           NT_DIMS, preferred_element_type=F32)
                    zero_rows = jnp.zeros((m, 2 * CHUNK), F32)
                    g = jnp.concatenate([part[(i // 2) * m:(i // 2 + 1) * m] if i % 2 else zero_rows
                                         for i in range(CHUNK // m)], axis=0)
                else:
                    z = jnp.where(((row // m) % 2) == 1, qt, kk) * decay
                    g = gram(z.astype(BF16))
                scores = jnp.where(lvl == int(np.log2(m)), g, scores)
            near = qt * fg * pltpu.roll(kk, 1, 0)
            diag = qt * kk
            per_head = lambda x: jnp.where(head0_cols,
                                           jnp.sum(x[:, :HGRN_DK], axis=1, keepdims=True),
                                           jnp.sum(x[:, HGRN_DK:], axis=1, keepdims=True))
            scores = jnp.where(lvl == 0, per_head(near), scores)
            out.append((o_inter, jnp.where(lvl == -1, per_head(diag), scores)))
        return out

    def output_and_state(j, staged, scored):
        rows = slice(j * CHUNK, (j + 1) * CHUNK)
        outs = []
        for p in pairs:
            _, _, v, kk, sums = staged[p]
            o_inter, scores = scored[p]
            outs.append(o_inter + jnp.dot(scores.astype(BF16),
                                          block_diag(v[:, :HGRN_DK], v[:, HGRN_DK:], zero_c),
                                          preferred_element_type=F32))
            b = sums[0:CHUNK]
            b_last = b[CHUNK - 1:CHUNK, :]
            kd = (kk * jnp.exp2(b_last - b)).astype(BF16)
            dec = jnp.exp2(b_last)
            for hh, sl in ((2 * p, slice(0, HGRN_DK)), (2 * p + 1, slice(HGRN_DK, pair_w))):
                upd = lax.dot_general(kd[:, sl], v[:, sl], TN_DIMS, preferred_element_type=F32)
                dec_rows = jnp.broadcast_to(dec[:, sl], (HGRN_DK, HGRN_DK)).T
                st_ref[hh] = st_ref[hh] * dec_rows + upd
        o_all = jnp.concatenate(outs, axis=1)
        a_ref[rows, :] = (_rmsnorm(o_all, gn_ref[...]) * gate_ref[rows, :].astype(F32)).astype(BF16)

    staged = load_and_sum(0)
    for j in range(n_chunks):
        scored = scores_and_inter(staged)
        nxt = load_and_sum(j + 1) if j + 1 < n_chunks else None
        output_and_state(j, staged, scored)
        staged = nxt


def _hgrn_scan(q, f, v, gate, g_norm, bsz, seq):
    n_tok = q.shape[0]
    tc = SCAN_TILE
    steps = seq // tc
    toks = pl.BlockSpec((tc, D_MODEL), lambda b, c: (b * steps + c, 0))
    dmat = jnp.asarray(_decay_sum_matrix(), BF16)
    lvl = jnp.asarray(np.tile(_level_index(), (1, 2)))
    return pl.pallas_call(
        functools.partial(_hgrn_scan_kernel, n_chunks=tc // CHUNK),
        out_shape=jax.ShapeDtypeStruct((n_tok, D_MODEL), BF16),
        grid=(bsz, steps),
        in_specs=[toks, toks, toks, toks, _resident((1, D_MODEL)), _resident(dmat.shape),
                  _resident(lvl.shape)],
        out_specs=toks,
        scratch_shapes=[pltpu.VMEM((HGRN_HEADS, HGRN_DK, HGRN_DK), F32)],
        compiler_params=_params("parallel", "arbitrary"),
        name="hgrn_scan",
    )(q, f, v, gate, g_norm, dmat, lvl)


def kernel(x, positions, mix_norm, mlp_norm, final_norm, attn_w_qkv, attn_b_qkv, attn_sinks,
           attn_w_o, hgrn_w_in, hgrn_g_norm, hgrn_w_o, hgrn_lower_bounds, mlp_w_up, mlp_w_down):
    bsz, seq, _ = x.shape
    n_tok = bsz * seq
    xf = x.reshape(n_tok, D_MODEL)
    gain = lambda g: g.reshape(1, D_MODEL).astype(F32)

    qt, k, vt = _qkv_proj(xf, positions, gain(mix_norm[0]), attn_w_qkv[0].astype(BF16),
                          attn_b_qkv[0].reshape(1, QKV_DIM).astype(F32))
    a, w_o0, w_up0, w_down0, w_in, w_o1, w_up1, w_down1 = _attention(
        qt, k, vt, attn_sinks[0].astype(F32), bsz, seq,
        [(attn_w_o, 0), (mlp_w_up, 0), (mlp_w_down, 0), (hgrn_w_in, 0), (hgrn_w_o, 0),
         (mlp_w_up, 1), (mlp_w_down, 1)])
    xf, h1 = _out_mlp(a, xf, w_o0, gain(mlp_norm[0]), w_up0, w_down0, gain(mix_norm[1]), last=False)

    hq, hf, hv, gate = _hgrn_in(h1, w_in, hgrn_lower_bounds.astype(F32))
    a = _hgrn_scan(hq, hf, hv, gate, gain(hgrn_g_norm[0]), bsz, seq)
    (out,) = _out_mlp(a, xf, w_o1, gain(mlp_norm[1]), w_up1, w_down1, gain(final_norm), last=True)
    return out.reshape(bsz, seq, D_MODEL)
```

```python
import functools

import numpy as np
import jax
import jax.numpy as jnp
from jax import lax
from jax.experimental import pallas as pl
from jax.experimental.pallas import tpu as pltpu

D_MODEL = 1024
HEAD_DIM = 64
N_Q_HEADS = 16
N_KV_HEADS = 4
GROUP = 4
Q_DIM = N_Q_HEADS * HEAD_DIM
KV_DIM = N_KV_HEADS * HEAD_DIM
QKV_DIM = Q_DIM + 2 * KV_DIM
WINDOW = 128
BLOCK = 128
ROT_DIM = 16
ROT_HALF = ROT_DIM // 2
ROPE_THETA = 500000.0
NEG_INF = -1e30
HGRN_HEADS = 8
HGRN_DK = 128
CHUNK = 64
D_FF = 4 * D_MODEL
FF_CHUNK = 1024
ROW_GROUPS = 2
AHEAD = 2

ROW_TILE = 1024
SCAN_TILE = 512
ATTN_Q_BLOCKS = 8
NORM_EPS = 1e-5
LOG2E = 1.4426950408889634

LANES = 128
BF16_SUBLANES = 16
VMEM_LIMIT = 56 * 1024 * 1024

BF16 = jnp.bfloat16
F32 = jnp.float32

NT_DIMS = (((1,), (1,)), ((), ()))
TN_DIMS = (((0,), (0,)), ((), ()))


def _rmsnorm(x, gain):
    ms = jnp.mean(x * x, axis=-1, keepdims=True)
    return x * lax.rsqrt(ms + NORM_EPS) * gain


def _params(*sem):
    return pltpu.CompilerParams(dimension_semantics=sem, vmem_limit_bytes=VMEM_LIMIT)


def _resident(shape):
    return pl.BlockSpec(shape, lambda *_: (0,) * len(shape))


def _cast_specs(weights, n_steps, step_index):
    in_specs, out_specs, shapes = [], [], []
    for w, layer in weights:
        _, n_rows, n_cols = w.shape
        rows = n_rows // n_steps
        in_specs.append(pl.BlockSpec((None, rows, n_cols),
                                     lambda *ids, layer=layer: (layer, step_index(*ids), 0)))
        out_specs.append(pl.BlockSpec((rows, n_cols), lambda *ids: (step_index(*ids), 0)))
        shapes.append(jax.ShapeDtypeStruct((n_rows, n_cols), BF16))
    return in_specs, out_specs, shapes


def _cast_slabs(in_refs, out_refs):
    for src, dst in zip(in_refs, out_refs, strict=True):
        dst[...] = src[...].astype(BF16)


def _qkv_kernel(x_ref, pos_ref, invf_ref, g_ref, w_ref, b_ref, qt_ref, k_ref, vt_ref):
    tm = x_ref.shape[0]
    qkv = jnp.concatenate(
        [jnp.dot(_rmsnorm(x_ref[r0:r0 + tm // ROW_GROUPS, :], g_ref[...]).astype(BF16), w_ref[...],
                 preferred_element_type=F32) for r0 in range(0, tm, tm // ROW_GROUPS)],
        axis=0) + b_ref[...]
    ang = pos_ref[...].astype(F32) * invf_ref[...]
    cos_t = jnp.cos(ang)
    sin_t = jnp.sin(ang)

    def rotate_t(tt):
        parts = []
        for hd in range(LANES // HEAD_DIM):
            base = hd * HEAD_DIM
            x1 = tt[base:base + ROT_HALF]
            x2 = tt[base + ROT_HALF:base + ROT_DIM]
            parts += [x1 * cos_t - x2 * sin_t, x2 * cos_t + x1 * sin_t,
                      tt[base + ROT_DIM:base + HEAD_DIM]]
        return jnp.concatenate(parts, axis=0)

    scale = HEAD_DIM ** -0.5 * LOG2E
    for j in range(Q_DIM // LANES):
        t = qkv[:, j * LANES:(j + 1) * LANES] * scale
        qt_ref[j * LANES:(j + 1) * LANES, :] = rotate_t(t.T).astype(BF16)
    for j in range(KV_DIM // LANES):
        t = qkv[:, Q_DIM + j * LANES:Q_DIM + (j + 1) * LANES]
        k_ref[:, j * LANES:(j + 1) * LANES] = rotate_t(t.T).T.astype(BF16)
        t = qkv[:, Q_DIM + KV_DIM + j * LANES:Q_DIM + KV_DIM + (j + 1) * LANES]
        vt_ref[j * LANES:(j + 1) * LANES, :] = t.T.astype(BF16)


def _qkv_proj(x, positions, gain, w_qkv, b_qkv):
    n_tok = x.shape[0]
    tm = ROW_TILE
    row = lambda w: pl.BlockSpec((tm, w), lambda i: (i, 0))
    col = lambda h: pl.BlockSpec((h, tm), lambda i: (0, i))
    inv_freq = ROPE_THETA ** (-jnp.arange(0, ROT_DIM, 2, dtype=F32) / ROT_DIM)
    return pl.pallas_call(
        _qkv_kernel,
        out_shape=[jax.ShapeDtypeStruct((Q_DIM, n_tok), BF16),
                   jax.ShapeDtypeStruct((n_tok, KV_DIM), BF16),
                   jax.ShapeDtypeStruct((KV_DIM, n_tok), BF16)],
        grid=(n_tok // tm,),
        in_specs=[row(D_MODEL), col(1), _resident((ROT_HALF, 1)), _resident((1, D_MODEL)),
                  _resident((D_MODEL, QKV_DIM)), _resident((1, QKV_DIM))],
        out_specs=[col(Q_DIM), row(KV_DIM), col(KV_DIM)],
        compiler_params=_params("parallel"),
        name="qkv_rope",
    )(x, positions.reshape(1, n_tok), inv_freq.reshape(ROT_HALF, 1), gain, w_qkv, b_qkv)


def _attn_kernel(sink_ref, qt_ref, kp_ref, kc_ref, vtp_ref, vtc_ref, *rest, q_blocks, n_cast):
    cast_in, o_ref, cast_out, bias_ref = rest[:n_cast], rest[n_cast], rest[n_cast + 1:-1], rest[-1]
    _cast_slabs(cast_in, cast_out)
    n = pl.program_id(1)
    band = 2 * BLOCK
    width = GROUP * BLOCK
    kj = lax.broadcasted_iota(jnp.int32, (band, width), 0)
    qi = lax.broadcasted_iota(jnp.int32, (band, width), 1) % BLOCK
    delta = qi + BLOCK - kj
    in_window = (delta >= 0) & (delta < WINDOW)
    bias_ref[0] = jnp.where(in_window & ((kj >= BLOCK) | (n > 0)), 0.0, NEG_INF)
    bias_ref[1] = jnp.where(in_window, 0.0, NEG_INF)
    head_of_lane = lax.broadcasted_iota(jnp.int32, (1, width), 1) // BLOCK
    zero_q = jnp.zeros((HEAD_DIM, width), BF16)
    ones_rows = jnp.ones((BF16_SUBLANES, band), BF16)

    def scores(j, kvh):
        tile = slice((kvh // 2) * LANES, (kvh // 2 + 1) * LANES)
        if j == 0:
            kband = jnp.concatenate([kp_ref[:, tile], kc_ref[:BLOCK, tile]], axis=0)
        else:
            kband = kc_ref[(j - 1) * BLOCK:(j + 1) * BLOCK, tile]
        qc = slice(j * BLOCK, (j + 1) * BLOCK)
        qt4 = jnp.concatenate(
            [qt_ref[(GROUP * kvh + g) * HEAD_DIM:(GROUP * kvh + g + 1) * HEAD_DIM, qc]
             for g in range(GROUP)], axis=1)
        rhs = jnp.concatenate([qt4, zero_q] if kvh % 2 == 0 else [zero_q, qt4], axis=0)
        s = jnp.dot(kband, rhs, preferred_element_type=F32)
        return s + bias_ref[0 if j == 0 else 1]

    def softmax_pv(j, kvh, s):
        hs = slice(kvh * HEAD_DIM, (kvh + 1) * HEAD_DIM)
        if j == 0:
            vt = jnp.concatenate([vtp_ref[hs, :], vtc_ref[hs, :BLOCK]], axis=1)
        else:
            vt = vtc_ref[hs, (j - 1) * BLOCK:(j + 1) * BLOCK]
        sink = jnp.zeros((1, width), F32)
        for g in range(GROUP):
            sink = jnp.where(head_of_lane == g, sink_ref[GROUP * kvh + g] * LOG2E, sink)
        m = jnp.maximum(jnp.max(s, axis=0, keepdims=True), sink)
        e = jnp.exp2(s - m).astype(BF16)
        pv = jnp.dot(jnp.concatenate([vt, ones_rows], axis=0), e,
                     preferred_element_type=F32)
        den = pv[HEAD_DIM:HEAD_DIM + 1, :] + jnp.exp2(sink - m)
        out_t = pv[:HEAD_DIM, :] * (1.0 / den)
        qr = slice(j * BLOCK, (j + 1) * BLOCK)
        for t in range(GROUP * HEAD_DIM // LANES):
            pair = jnp.concatenate([out_t[:, (2 * t) * BLOCK:(2 * t + 1) * BLOCK],
                                    out_t[:, (2 * t + 1) * BLOCK:(2 * t + 2) * BLOCK]], axis=0)
            o_ref[qr, (2 * kvh + t) * LANES:(2 * kvh + t + 1) * LANES] = pair.T.astype(BF16)

    items = [(j, kvh) for j in range(q_blocks) for kvh in range(N_KV_HEADS)]
    pending = [scores(*item) for item in items[:AHEAD]]
    for i, item in enumerate(items):
        if i + AHEAD < len(items):
            pending.append(scores(*items[i + AHEAD]))
        softmax_pv(*item, pending.pop(0))


def _attention(qt, k, vt, sinks, bsz, seq, cast_weights):
    q_blocks = ATTN_Q_BLOCKS
    tq = q_blocks * BLOCK
    steps = seq // tq
    cast_in, cast_out, cast_shapes = _cast_specs(cast_weights, bsz * steps, lambda b, n: b * steps + n)
    nb = seq // BLOCK
    cur = lambda b, n: (b * steps + n, 0)
    prev = lambda b, n: (b * nb + jnp.maximum(n * q_blocks - 1, 0), 0)
    cur_t = lambda b, n: (0, b * steps + n)
    prev_t = lambda b, n: (0, b * nb + jnp.maximum(n * q_blocks - 1, 0))
    return pl.pallas_call(
        functools.partial(_attn_kernel, q_blocks=q_blocks, n_cast=len(cast_weights)),
        out_shape=[jax.ShapeDtypeStruct((bsz * seq, Q_DIM), BF16)] + cast_shapes,
        grid=(bsz, steps),
        in_specs=[pl.BlockSpec(memory_space=pltpu.SMEM),
                  pl.BlockSpec((Q_DIM, tq), cur_t),
                  pl.BlockSpec((BLOCK, KV_DIM), prev), pl.BlockSpec((tq, KV_DIM), cur),
                  pl.BlockSpec((KV_DIM, BLOCK), prev_t), pl.BlockSpec((KV_DIM, tq), cur_t)] + cast_in,
        out_specs=[pl.BlockSpec((tq, Q_DIM), cur)] + cast_out,
        scratch_shapes=[pltpu.VMEM((2, 2 * BLOCK, GROUP * BLOCK), F32)],
        compiler_params=_params("parallel", "parallel"),
        name="swa_attention",
    )(sinks, qt, k, k, vt, vt, *[w for w, _ in cast_weights])


def _out_mlp_kernel(a_ref, x_ref, wo_ref, g_ref, wu_ref, wd_ref, gn_ref, *out_refs, last):
    tm = x_ref.shape[0]
    groups = [slice(r0, r0 + tm // ROW_GROUPS) for r0 in range(0, tm, tm // ROW_GROUPS)]
    x1 = [x_ref[r, :] + jnp.dot(a_ref[r, :], wo_ref[...], preferred_element_type=F32) for r in groups]
    h = [_rmsnorm(v, g_ref[...]).astype(BF16) for v in x1]
    x2 = x1
    for c in range(D_FF // FF_CHUNK):
        cols = slice(c * FF_CHUNK, (c + 1) * FF_CHUNK)
        u = [jnp.maximum(jnp.dot(v, wu_ref[:, cols], preferred_element_type=F32), 0.0) for v in h]
        x2 = [acc + jnp.dot((v * v).astype(BF16), wd_ref[cols, :], preferred_element_type=F32)
              for acc, v in zip(x2, u, strict=True)]
    for r, v in zip(groups, x2, strict=True):
        normed = _rmsnorm(v, gn_ref[...])
        if last:
            out_refs[0][r, :] = normed
        else:
            out_refs[0][r, :] = v
            out_refs[1][r, :] = normed.astype(BF16)


def _out_mlp(a, x, w_o, gain, w_up, w_down, next_gain, last):
    n_tok = x.shape[0]
    tm = ROW_TILE
    row = pl.BlockSpec((tm, D_MODEL), lambda i: (i, 0))
    once = pl.Buffered(1)
    wspec = lambda shape: pl.BlockSpec(shape, lambda i: (0, 0), pipeline_mode=once)
    stream = jax.ShapeDtypeStruct((n_tok, D_MODEL), F32)
    return pl.pallas_call(
        functools.partial(_out_mlp_kernel, last=last),
        out_shape=[stream] if last else [stream, jax.ShapeDtypeStruct((n_tok, D_MODEL), BF16)],
        grid=(n_tok // tm,),
        in_specs=[row, row, wspec((D_MODEL, D_MODEL)), wspec((1, D_MODEL)),
                  wspec((D_MODEL, D_FF)), wspec((D_FF, D_MODEL)), wspec((1, D_MODEL))],
        out_specs=[row] if last else [row, row],
        compiler_params=_params("parallel"),
        name="out_proj_mlp",
    )(a, x, w_o, gain, w_up, w_down, next_gain)


def _hgrn_in_kernel(h_ref, w_ref, lbp_ref, q_ref, f_ref, v_ref, gate_ref):
    h = h_ref[...]
    lbp = lbp_ref[...]
    e = jnp.exp(lbp - jnp.max(lbp, axis=0, keepdims=True))
    sm = e / jnp.sum(e, axis=0, keepdims=True)
    lb = (sm[0:1] + sm[1:2]) - sm[0:1]

    def proj(c):
        return jnp.dot(h, w_ref[:, c * D_MODEL:(c + 1) * D_MODEL], preferred_element_type=F32)

    q = proj(0)
    f = proj(1)
    q_ref[...] = (q * jax.nn.sigmoid(q)).astype(BF16)
    g = proj(3)
    f_ref[...] = lb + (1.0 - lb) * jax.nn.sigmoid(f)
    i = proj(2)
    gate_ref[...] = (g * jax.nn.sigmoid(g)).astype(BF16)
    v_ref[...] = i.astype(BF16)


def _hgrn_in(h, w_in, lb_params):
    n_tok = h.shape[0]
    tm = ROW_TILE
    row = pl.BlockSpec((tm, D_MODEL), lambda i: (i, 0))
    once = pl.Buffered(1)
    out = lambda dtype: jax.ShapeDtypeStruct((n_tok, D_MODEL), dtype)
    return pl.pallas_call(
        _hgrn_in_kernel,
        out_shape=[out(BF16), out(F32), out(BF16), out(BF16)],
        grid=(n_tok // tm,),
        in_specs=[row,
                  pl.BlockSpec((D_MODEL, 4 * D_MODEL), lambda i: (0, 0), pipeline_mode=once),
                  _resident((2, D_MODEL))],
        out_specs=[row, row, row, row],
        compiler_params=_params("parallel"),
        name="hgrn_in_proj",
    )(h, w_in, lb_params)


LEVELS = (32, 16, 8, 4, 2)
SPLIT = 2


def _decay_sum_matrix():
    t = np.arange(CHUNK)[:, None]
    u = np.arange(CHUNK)[None, :]
    mats = [u <= t]
    for m in LEVELS:
        blk = t // m
        odd = (blk % 2) == 1
        q_side = (u >= blk * m) & (u <= t)
        k_side = (u > t) & (u < (blk + 1) * m)
        mats.append(np.where(odd, q_side, k_side))
    d = np.concatenate(mats, axis=0).astype(np.float32)
    return np.concatenate([d] * SPLIT, axis=1)


def _level_index():
    t = np.arange(CHUNK)[:, None]
    s = np.arange(CHUNK)[None, :]
    x = t ^ s
    lvl = np.floor(np.log2(np.maximum(x, 1))).astype(np.int32)
    lvl = np.where(t == s, -1, lvl)
    return np.where(t < s, -2, lvl).astype(np.int32)


def _hgrn_scan_kernel(q_ref, f_ref, v_ref, gate_ref, gn_ref, dmat_ref, lvl_ref, a_ref, st_ref,
                      *, n_chunks):
    @pl.when(pl.program_id(1) == 0)
    def _():
        st_ref[...] = jnp.zeros_like(st_ref)

    pair_w = 2 * HGRN_DK
    row = lax.broadcasted_iota(jnp.int32, (CHUNK, pair_w), 0)
    lvl = lvl_ref[...]
    head0_cols = lax.broadcasted_iota(jnp.int32, (CHUNK, 2 * CHUNK), 1) < CHUNK
    dmat = dmat_ref[...]
    zero_c = jnp.zeros((CHUNK, HGRN_DK), BF16)
    zero_s = jnp.zeros((HGRN_DK, HGRN_DK), BF16)

    def block_diag(a0, a1, zero):
        return jnp.concatenate([jnp.concatenate([a0, zero], axis=1),
                                jnp.concatenate([zero, a1], axis=1)], axis=0)

    pairs = range(HGRN_HEADS // 2)

    def gram(z):
        zb = block_diag(z[:, :HGRN_DK], z[:, HGRN_DK:], zero_c)
        return lax.dot_general(z, zb, NT_DIMS, preferred_element_type=F32)

    def load_and_sum(j):
        rows = slice(j * CHUNK, (j + 1) * CHUNK)
        out = []
        for p in pairs:
            cols = slice(p * pair_w, (p + 1) * pair_w)
            qt, fg, v = q_ref[rows, cols].astype(F32), f_ref[rows, cols], v_ref[rows, cols]
            lf = jnp.log2(fg)
            pieces, rest = [], lf
            for _ in range(SPLIT):
                piece = rest.astype(BF16)
                pieces.append(piece)
                rest = rest - piece.astype(F32)
            sums = jnp.dot(dmat, jnp.concatenate(pieces, axis=0), preferred_element_type=F32)
            out.append((qt, fg, v, 1.0 - fg, sums))
        return out

    def scores_and_inter(staged):
        out = []
        for p in pairs:
            qt, fg, _, kk, sums = staged[p]
            qe = (qt * jnp.exp2(sums[0:CHUNK])).astype(BF16)
            st_bd = block_diag(st_ref[2 * p].astype(BF16), st_ref[2 * p + 1].astype(BF16), zero_s)
            o_inter = jnp.dot(qe, st_bd, preferred_element_type=F32)
            scores = jnp.zeros((CHUNK, 2 * CHUNK), F32)
            for li, m in enumerate(LEVELS):
                decay = jnp.exp2(sums[(1 + li) * CHUNK:(2 + li) * CHUNK])
                if m >= 8:
                    blocks = [slice(i * m, (i + 1) * m) for i in range(CHUNK // m)]
                    zq = jnp.concatenate([qt[b] * decay[b] for b in blocks[1::2]], axis=0)
                    zk = jnp.concatenate([kk[b] * decay[b] if i % 2 == 0 else jnp.zeros((m, pair_w), F32)
                                          for i, b in enumerate(blocks)], axis=0).astype(BF16)
                    part = lax.dot_general(zq.astype(BF16),
                                           block_diag(zk[:, :HGRN_DK], zk[:, HGRN_DK:], zero_c),
                                           NT_DIMS, preferred_element_type=F32)
                    zero_rows = jnp.zeros((m, 2 * CHUNK), F32)
                    g = jnp.concatenate([part[(i // 2) * m:(i // 2 + 1) * m] if i % 2 else zero_rows
                                         for i in range(CHUNK // m)], axis=0)
                else:
                    z = jnp.where(((row // m) % 2) == 1, qt, kk) * decay
                    g = gram(z.astype(BF16))
                scores = jnp.where(lvl == int(np.log2(m)), g, scores)
            near = qt * fg * pltpu.roll(kk, 1, 0)
            diag = qt * kk
            per_head = lambda x: jnp.where(head0_cols,
                                           jnp.sum(x[:, :HGRN_DK], axis=1, keepdims=True),
                                           jnp.sum(x[:, HGRN_DK:], axis=1, keepdims=True))
            scores = jnp.where(lvl == 0, per_head(near), scores)
            out.append((o_inter, jnp.where(lvl == -1, per_head(diag), scores)))
        return out

    def output_and_state(j, staged, scored):
        rows = slice(j * CHUNK, (j + 1) * CHUNK)
        outs = []
        for p in pairs:
            _, _, v, kk, sums = staged[p]
            o_inter, scores = scored[p]
            outs.append(o_inter + jnp.dot(scores.astype(BF16),
                                          block_diag(v[:, :HGRN_DK], v[:, HGRN_DK:], zero_c),
                                          preferred_element_type=F32))
            b = sums[0:CHUNK]
            b_last = b[CHUNK - 1:CHUNK, :]
            kd = (kk * jnp.exp2(b_last - b)).astype(BF16)
            dec = jnp.exp2(b_last)
            for hh, sl in ((2 * p, slice(0, HGRN_DK)), (2 * p + 1, slice(HGRN_DK, pair_w))):
                upd = lax.dot_general(kd[:, sl], v[:, sl], TN_DIMS, preferred_element_type=F32)
                dec_rows = jnp.broadcast_to(dec[:, sl], (HGRN_DK, HGRN_DK)).T
                st_ref[hh] = st_ref[hh] * dec_rows + upd
        o_all = jnp.concatenate(outs, axis=1)
        a_ref[rows, :] = (_rmsnorm(o_all, gn_ref[...]) * gate_ref[rows, :].astype(F32)).astype(BF16)

    staged = load_and_sum(0)
    for j in range(n_chunks):
        scored = scores_and_inter(staged)
        nxt = load_and_sum(j + 1) if j + 1 < n_chunks else None
        output_and_state(j, staged, scored)
        staged = nxt


def _hgrn_scan(q, f, v, gate, g_norm, bsz, seq):
    n_tok = q.shape[0]
    tc = SCAN_TILE
    steps = seq // tc
    toks = pl.BlockSpec((tc, D_MODEL), lambda b, c: (b * steps + c, 0))
    dmat = jnp.asarray(_decay_sum_matrix(), BF16)
    lvl = jnp.asarray(np.tile(_level_index(), (1, 2)))
    return pl.pallas_call(
        functools.partial(_hgrn_scan_kernel, n_chunks=tc // CHUNK),
        out_shape=jax.ShapeDtypeStruct((n_tok, D_MODEL), BF16),
        grid=(bsz, steps),
        in_specs=[toks, toks, toks, toks, _resident((1, D_MODEL)), _resident(dmat.shape),
                  _resident(lvl.shape)],
        out_specs=toks,
        scratch_shapes=[pltpu.VMEM((HGRN_HEADS, HGRN_DK, HGRN_DK), F32)],
        compiler_params=_params("parallel", "arbitrary"),
        name="hgrn_scan",
    )(q, f, v, gate, g_norm, dmat, lvl)


def kernel(x, positions, mix_norm, mlp_norm, final_norm, attn_w_qkv, attn_b_qkv, attn_sinks,
           attn_w_o, hgrn_w_in, hgrn_g_norm, hgrn_w_o, hgrn_lower_bounds, mlp_w_up, mlp_w_down):
    bsz, seq, _ = x.shape
    n_tok = bsz * seq
    xf = x.reshape(n_tok, D_MODEL)
    gain = lambda g: g.reshape(1, D_MODEL).astype(F32)

    qt, k, vt = _qkv_proj(xf, positions, gain(mix_norm[0]), attn_w_qkv[0].astype(BF16),
                          attn_b_qkv[0].reshape(1, QKV_DIM).astype(F32))
    a, w_o0, w_up0, w_down0, w_in, w_o1, w_up1, w_down1 = _attention(
        qt, k, vt, attn_sinks[0].astype(F32), bsz, seq,
        [(attn_w_o, 0), (mlp_w_up, 0), (mlp_w_down, 0), (hgrn_w_in, 0), (hgrn_w_o, 0),
         (mlp_w_up, 1), (mlp_w_down, 1)])
    xf, h1 = _out_mlp(a, xf, w_o0, gain(mlp_norm[0]), w_up0, w_down0, gain(mix_norm[1]), last=False)

    hq, hf, hv, gate = _hgrn_in(h1, w_in, hgrn_lower_bounds.astype(F32))
    a = _hgrn_scan(hq, hf, hv, gate, gain(hgrn_g_norm[0]), bsz, seq)
    (out,) = _out_mlp(a, xf, w_o1, gain(mlp_norm[1]), w_up1, w_down1, gain(final_norm), last=True)
    return out.reshape(bsz, seq, D_MODEL)
```

```python
import functools

import numpy as np
import jax
import jax.numpy as jnp
from jax import lax
from jax.experimental import pallas as pl
from jax.experimental.pallas import tpu as pltpu

D_MODEL = 1024
HEAD_DIM = 64
N_Q_HEADS = 16
N_KV_HEADS = 4
GROUP = 4
Q_DIM = N_Q_HEADS * HEAD_DIM
KV_DIM = N_KV_HEADS * HEAD_DIM
QKV_DIM = Q_DIM + 2 * KV_DIM
WINDOW = 128
BLOCK = 128
ROT_DIM = 16
ROT_HALF = ROT_DIM // 2
ROPE_THETA = 500000.0
NEG_INF = -1e30
HGRN_HEADS = 8
HGRN_DK = 128
CHUNK = 64
D_FF = 4 * D_MODEL
FF_CHUNK = 1024
AHEAD = 2

ROW_TILE = 1024
SCAN_TILE = 512
ATTN_Q_BLOCKS = 8
IN_BUFFERS = 3
NORM_EPS = 1e-5
LOG2E = 1.4426950408889634

LANES = 128
BF16_SUBLANES = 16
VMEM_LIMIT = 56 * 1024 * 1024

BF16 = jnp.bfloat16
F32 = jnp.float32

NT_DIMS = (((1,), (1,)), ((), ()))
TN_DIMS = (((0,), (0,)), ((), ()))


def _rmsnorm(x, gain):
    ms = jnp.mean(x * x, axis=-1, keepdims=True)
    return x * lax.rsqrt(ms + NORM_EPS) * gain


def _params(*sem):
    return pltpu.CompilerParams(dimension_semantics=sem, vmem_limit_bytes=VMEM_LIMIT)


def _resident(shape):
    return pl.BlockSpec(shape, lambda *_: (0,) * len(shape))


def _cast_specs(weights, n_steps, step_index):
    in_specs, out_specs, shapes = [], [], []
    for w, layer in weights:
        _, n_rows, n_cols = w.shape
        rows = n_rows // n_steps
        in_specs.append(pl.BlockSpec((None, rows, n_cols),
                                     lambda *ids, layer=layer: (layer, step_index(*ids), 0)))
        out_specs.append(pl.BlockSpec((rows, n_cols), lambda *ids: (step_index(*ids), 0)))
        shapes.append(jax.ShapeDtypeStruct((n_rows, n_cols), BF16))
    return in_specs, out_specs, shapes


def _cast_slabs(in_refs, out_refs):
    for src, dst in zip(in_refs, out_refs, strict=True):
        dst[...] = src[...].astype(BF16)


def _qkv_kernel(x_ref, pos_ref, invf_ref, g_ref, w_ref, b_ref, qt_ref, k_ref, vt_ref):
    h = _rmsnorm(x_ref[...], g_ref[...]).astype(BF16)
    qkv = jnp.dot(h, w_ref[...], preferred_element_type=F32) + b_ref[...]
    ang = pos_ref[...].astype(F32) * invf_ref[...]
    cos_t = jnp.cos(ang)
    sin_t = jnp.sin(ang)

    def rotate_t(tt):
        parts = []
        for hd in range(LANES // HEAD_DIM):
            base = hd * HEAD_DIM
            x1 = tt[base:base + ROT_HALF]
            x2 = tt[base + ROT_HALF:base + ROT_DIM]
            parts += [x1 * cos_t - x2 * sin_t, x2 * cos_t + x1 * sin_t,
                      tt[base + ROT_DIM:base + HEAD_DIM]]
        return jnp.concatenate(parts, axis=0)

    scale = HEAD_DIM ** -0.5 * LOG2E
    for j in range(Q_DIM // LANES):
        t = qkv[:, j * LANES:(j + 1) * LANES] * scale
        qt_ref[j * LANES:(j + 1) * LANES, :] = rotate_t(t.T).astype(BF16)
    for j in range(KV_DIM // LANES):
        t = qkv[:, Q_DIM + j * LANES:Q_DIM + (j + 1) * LANES]
        k_ref[:, j * LANES:(j + 1) * LANES] = rotate_t(t.T).T.astype(BF16)
        t = qkv[:, Q_DIM + KV_DIM + j * LANES:Q_DIM + KV_DIM + (j + 1) * LANES]
        vt_ref[j * LANES:(j + 1) * LANES, :] = t.T.astype(BF16)


def _qkv_proj(x, positions, gain, w_qkv, b_qkv):
    n_tok = x.shape[0]
    tm = ROW_TILE
    row = lambda w: pl.BlockSpec((tm, w), lambda i: (i, 0))
    col = lambda h: pl.BlockSpec((h, tm), lambda i: (0, i))
    inv_freq = ROPE_THETA ** (-jnp.arange(0, ROT_DIM, 2, dtype=F32) / ROT_DIM)
    return pl.pallas_call(
        _qkv_kernel,
        out_shape=[jax.ShapeDtypeStruct((Q_DIM, n_tok), BF16),
                   jax.ShapeDtypeStruct((n_tok, KV_DIM), BF16),
                   jax.ShapeDtypeStruct((KV_DIM, n_tok), BF16)],
        grid=(n_tok // tm,),
        in_specs=[row(D_MODEL), col(1), _resident((ROT_HALF, 1)), _resident((1, D_MODEL)),
                  _resident((D_MODEL, QKV_DIM)), _resident((1, QKV_DIM))],
        out_specs=[col(Q_DIM), row(KV_DIM), col(KV_DIM)],
        compiler_params=_params("parallel"),
        name="qkv_rope",
    )(x, positions.reshape(1, n_tok), inv_freq.reshape(ROT_HALF, 1), gain, w_qkv, b_qkv)


def _attn_kernel(sink_ref, qt_ref, kp_ref, kc_ref, vtp_ref, vtc_ref, *rest, q_blocks, n_cast):
    cast_in, o_ref, cast_out, bias_ref = rest[:n_cast], rest[n_cast], rest[n_cast + 1:-1], rest[-1]
    _cast_slabs(cast_in, cast_out)
    n = pl.program_id(1)
    band = 2 * BLOCK
    width = GROUP * BLOCK
    kj = lax.broadcasted_iota(jnp.int32, (band, width), 0)
    qi = lax.broadcasted_iota(jnp.int32, (band, width), 1) % BLOCK
    delta = qi + BLOCK - kj
    in_window = (delta >= 0) & (delta < WINDOW)
    bias_ref[0] = jnp.where(in_window & ((kj >= BLOCK) | (n > 0)), 0.0, NEG_INF)
    bias_ref[1] = jnp.where(in_window, 0.0, NEG_INF)
    head_of_lane = lax.broadcasted_iota(jnp.int32, (1, width), 1) // BLOCK
    zero_q = jnp.zeros((HEAD_DIM, width), BF16)
    ones_rows = jnp.ones((BF16_SUBLANES, band), BF16)

    def scores(j, kvh):
        tile = slice((kvh // 2) * LANES, (kvh // 2 + 1) * LANES)
        if j == 0:
            kband = jnp.concatenate([kp_ref[:, tile], kc_ref[:BLOCK, tile]], axis=0)
        else:
            kband = kc_ref[(j - 1) * BLOCK:(j + 1) * BLOCK, tile]
        qc = slice(j * BLOCK, (j + 1) * BLOCK)
        qt4 = jnp.concatenate(
            [qt_ref[(GROUP * kvh + g) * HEAD_DIM:(GROUP * kvh + g + 1) * HEAD_DIM, qc]
             for g in range(GROUP)], axis=1)
        rhs = jnp.concatenate([qt4, zero_q] if kvh % 2 == 0 else [zero_q, qt4], axis=0)
        s = jnp.dot(kband, rhs, preferred_element_type=F32)
        return s + bias_ref[0 if j == 0 else 1]

    def softmax_pv(j, kvh, s):
        hs = slice(kvh * HEAD_DIM, (kvh + 1) * HEAD_DIM)
        if j == 0:
            vt = jnp.concatenate([vtp_ref[hs, :], vtc_ref[hs, :BLOCK]], axis=1)
        else:
            vt = vtc_ref[hs, (j - 1) * BLOCK:(j + 1) * BLOCK]
        sink = jnp.zeros((1, width), F32)
        for g in range(GROUP):
            sink = jnp.where(head_of_lane == g, sink_ref[GROUP * kvh + g] * LOG2E, sink)
        m = jnp.maximum(jnp.max(s, axis=0, keepdims=True), sink)
        e = jnp.exp2(s - m).astype(BF16)
        pv = jnp.dot(jnp.concatenate([vt, ones_rows], axis=0), e,
                     preferred_element_type=F32)
        den = pv[HEAD_DIM:HEAD_DIM + 1, :] + jnp.exp2(sink - m)
        out_t = pv[:HEAD_DIM, :] * (1.0 / den)
        qr = slice(j * BLOCK, (j + 1) * BLOCK)
        for t in range(GROUP * HEAD_DIM // LANES):
            pair = jnp.concatenate([out_t[:, (2 * t) * BLOCK:(2 * t + 1) * BLOCK],
                                    out_t[:, (2 * t + 1) * BLOCK:(2 * t + 2) * BLOCK]], axis=0)
            o_ref[qr, (2 * kvh + t) * LANES:(2 * kvh + t + 1) * LANES] = pair.T.astype(BF16)

    items = [(j, kvh) for j in range(q_blocks) for kvh in range(N_KV_HEADS)]
    pending = [scores(*item) for item in items[:AHEAD]]
    for i, item in enumerate(items):
        if i + AHEAD < len(items):
            pending.append(scores(*items[i + AHEAD]))
        softmax_pv(*item, pending.pop(0))


def _attention(qt, k, vt, sinks, bsz, seq, cast_weights):
    q_blocks = ATTN_Q_BLOCKS
    tq = q_blocks * BLOCK
    steps = seq // tq
    cast_in, cast_out, cast_shapes = _cast_specs(cast_weights, bsz * steps, lambda b, n: b * steps + n)
    nb = seq // BLOCK
    cur = lambda b, n: (b * steps + n, 0)
    prev = lambda b, n: (b * nb + jnp.maximum(n * q_blocks - 1, 0), 0)
    cur_t = lambda b, n: (0, b * steps + n)
    prev_t = lambda b, n: (0, b * nb + jnp.maximum(n * q_blocks - 1, 0))
    return pl.pallas_call(
        functools.partial(_attn_kernel, q_blocks=q_blocks, n_cast=len(cast_weights)),
        out_shape=[jax.ShapeDtypeStruct((bsz * seq, Q_DIM), BF16)] + cast_shapes,
        grid=(bsz, steps),
        in_specs=[pl.BlockSpec(memory_space=pltpu.SMEM),
                  pl.BlockSpec((Q_DIM, tq), cur_t),
                  pl.BlockSpec((BLOCK, KV_DIM), prev), pl.BlockSpec((tq, KV_DIM), cur),
                  pl.BlockSpec((KV_DIM, BLOCK), prev_t), pl.BlockSpec((KV_DIM, tq), cur_t)] + cast_in,
        out_specs=[pl.BlockSpec((tq, Q_DIM), cur)] + cast_out,
        scratch_shapes=[pltpu.VMEM((2, 2 * BLOCK, GROUP * BLOCK), F32)],
        compiler_params=_params("parallel", "parallel"),
        name="swa_attention",
    )(sinks, qt, k, k, vt, vt, *[w for w, _ in cast_weights])


def _out_mlp_kernel(a_ref, x_ref, wo_ref, g_ref, wu_ref, wd_ref, gn_ref, *out_refs, last):
    x1 = x_ref[...] + jnp.dot(a_ref[...], wo_ref[...], preferred_element_type=F32)
    h = _rmsnorm(x1, g_ref[...]).astype(BF16)
    x2 = x1
    for c in range(D_FF // FF_CHUNK):
        cols = slice(c * FF_CHUNK, (c + 1) * FF_CHUNK)
        u = jnp.maximum(jnp.dot(h, wu_ref[:, cols], preferred_element_type=F32), 0.0)
        x2 = x2 + jnp.dot((u * u).astype(BF16), wd_ref[cols, :], preferred_element_type=F32)
    normed = _rmsnorm(x2, gn_ref[...])
    if last:
        out_refs[0][...] = normed
    else:
        out_refs[0][...] = x2
        out_refs[1][...] = normed.astype(BF16)


def _out_mlp(a, x, w_o, gain, w_up, w_down, next_gain, last):
    n_tok = x.shape[0]
    tm = ROW_TILE
    row = pl.BlockSpec((tm, D_MODEL), lambda i: (i, 0))
    once = pl.Buffered(1)
    wspec = lambda shape: pl.BlockSpec(shape, lambda i: (0, 0), pipeline_mode=once)
    stream = jax.ShapeDtypeStruct((n_tok, D_MODEL), F32)
    return pl.pallas_call(
        functools.partial(_out_mlp_kernel, last=last),
        out_shape=[stream] if last else [stream, jax.ShapeDtypeStruct((n_tok, D_MODEL), BF16)],
        grid=(n_tok // tm,),
        in_specs=[row, row, wspec((D_MODEL, D_MODEL)), wspec((1, D_MODEL)),
                  wspec((D_MODEL, D_FF)), wspec((D_FF, D_MODEL)), wspec((1, D_MODEL))],
        out_specs=[row] if last else [row, row],
        compiler_params=_params("parallel"),
        name="out_proj_mlp",
    )(a, x, w_o, gain, w_up, w_down, next_gain)


def _hgrn_in_kernel(h_ref, w_ref, lbp_ref, q_ref, f_ref, v_ref, gate_ref):
    h = h_ref[...]
    lbp = lbp_ref[...]
    e = jnp.exp(lbp - jnp.max(lbp, axis=0, keepdims=True))
    sm = e / jnp.sum(e, axis=0, keepdims=True)
    lb = (sm[0:1] + sm[1:2]) - sm[0:1]

    def proj(c):
        return jnp.dot(h, w_ref[:, c * D_MODEL:(c + 1) * D_MODEL], preferred_element_type=F32)

    q = proj(0)
    f = proj(1)
    q_ref[...] = (q * jax.nn.sigmoid(q)).astype(BF16)
    g = proj(3)
    f_ref[...] = lb + (1.0 - lb) * jax.nn.sigmoid(f)
    i = proj(2)
    gate_ref[...] = (g * jax.nn.sigmoid(g)).astype(BF16)
    v_ref[...] = i.astype(BF16)


def _hgrn_in(h, w_in, lb_params):
    n_tok = h.shape[0]
    tm = ROW_TILE
    row = pl.BlockSpec((tm, D_MODEL), lambda i: (i, 0))
    row_in = pl.BlockSpec((tm, D_MODEL), lambda i: (i, 0), pipeline_mode=pl.Buffered(IN_BUFFERS))
    out = lambda dtype: jax.ShapeDtypeStruct((n_tok, D_MODEL), dtype)

    def outer(h_hbm, w_ref, lbp_ref, q_hbm, f_hbm, v_hbm, gate_hbm):
        def step(h_ref, q_ref, f_ref, v_ref, gate_ref):
            _hgrn_in_kernel(h_ref, w_ref, lbp_ref, q_ref, f_ref, v_ref, gate_ref)

        pltpu.emit_pipeline(step, grid=(n_tok // tm,), in_specs=[row_in],
                            out_specs=[row, row, row, row])(h_hbm, q_hbm, f_hbm, v_hbm, gate_hbm)

    whole = pl.BlockSpec(memory_space=pltpu.VMEM)
    stream = pl.BlockSpec(memory_space=pl.ANY)
    return pl.pallas_call(
        outer,
        out_shape=[out(BF16), out(F32), out(BF16), out(BF16)],
        in_specs=[stream, whole, whole],
        out_specs=[stream, stream, stream, stream],
        compiler_params=pltpu.CompilerParams(vmem_limit_bytes=VMEM_LIMIT),
        name="hgrn_in_proj",
    )(h, w_in, lb_params)


LEVELS = (32, 16, 8, 4, 2)
SPLIT = 2


def _decay_sum_matrix():
    t = np.arange(CHUNK)[:, None]
    u = np.arange(CHUNK)[None, :]
    mats = [u <= t]
    for m in LEVELS:
        blk = t // m
        odd = (blk % 2) == 1
        q_side = (u >= blk * m) & (u <= t)
        k_side = (u > t) & (u < (blk + 1) * m)
        mats.append(np.where(odd, q_side, k_side))
    d = np.concatenate(mats, axis=0).astype(np.float32)
    return np.concatenate([d] * SPLIT, axis=1)


def _level_index():
    t = np.arange(CHUNK)[:, None]
    s = np.arange(CHUNK)[None, :]
    x = t ^ s
    lvl = np.floor(np.log2(np.maximum(x, 1))).astype(np.int32)
    lvl = np.where(t == s, -1, lvl)
    return np.where(t < s, -2, lvl).astype(np.int32)


def _hgrn_scan_kernel(q_ref, f_ref, v_ref, gate_ref, gn_ref, dmat_ref, lvl_ref, a_ref, st_ref,
                      *, n_chunks):
    @pl.when(pl.program_id(1) == 0)
    def _():
        st_ref[...] = jnp.zeros_like(st_ref)

    pair_w = 2 * HGRN_DK
    row = lax.broadcasted_iota(jnp.int32, (CHUNK, pair_w), 0)
    lvl = lvl_ref[...]
    head0_cols = lax.broadcasted_iota(jnp.int32, (CHUNK, 2 * CHUNK), 1) < CHUNK
    dmat = dmat_ref[...]
    zero_c = jnp.zeros((CHUNK, HGRN_DK), BF16)
    zero_s = jnp.zeros((HGRN_DK, HGRN_DK), BF16)

    def block_diag(a0, a1, zero):
        return jnp.concatenate([jnp.concatenate([a0, zero], axis=1),
                                jnp.concatenate([zero, a1], axis=1)], axis=0)

    pairs = range(HGRN_HEADS // 2)

    def gram(z):
        zb = block_diag(z[:, :HGRN_DK], z[:, HGRN_DK:], zero_c)
        return lax.dot_general(z, zb, NT_DIMS, preferred_element_type=F32)

    def load_and_sum(j):
        rows = slice(j * CHUNK, (j + 1) * CHUNK)
        out = []
        for p in pairs:
            cols = slice(p * pair_w, (p + 1) * pair_w)
            qt, fg, v = q_ref[rows, cols].astype(F32), f_ref[rows, cols], v_ref[rows, cols]
            lf = jnp.log2(fg)
            pieces, rest = [], lf
            for _ in range(SPLIT):
                piece = rest.astype(BF16)
                pieces.append(piece)
                rest = rest - piece.astype(F32)
            sums = jnp.dot(dmat, jnp.concatenate(pieces, axis=0), preferred_element_type=F32)
            out.append((qt, fg, v, 1.0 - fg, sums))
        return out

    def scores_and_inter(staged):
        out = []
        for p in pairs:
            qt, fg, _, kk, sums = staged[p]
            qe = (qt * jnp.exp2(sums[0:CHUNK])).astype(BF16)
            st_bd = block_diag(st_ref[2 * p].astype(BF16), st_ref[2 * p + 1].astype(BF16), zero_s)
            o_inter = jnp.dot(qe, st_bd, preferred_element_type=F32)
            scores = jnp.zeros((CHUNK, 2 * CHUNK), F32)
            for li, m in enumerate(LEVELS):
                decay = jnp.exp2(sums[(1 + li) * CHUNK:(2 + li) * CHUNK])
                if m >= 8:
                    blocks = [slice(i * m, (i + 1) * m) for i in range(CHUNK // m)]
                    zq = jnp.concatenate([qt[b] * decay[b] for b in blocks[1::2]], axis=0)
                    zk = jnp.concatenate([kk[b] * decay[b] if i % 2 == 0 else jnp.zeros((m, pair_w), F32)
                                          for i, b in enumerate(blocks)], axis=0).astype(BF16)
                    part = lax.dot_general(zq.astype(BF16),
                                           block_diag(zk[:, :HGRN_DK], zk[:, HGRN_DK:], zero_c),
                                           NT_DIMS, preferred_element_type=F32)
                    zero_rows = jnp.zeros((m, 2 * CHUNK), F32)
                    g = jnp.concatenate([part[(i // 2) * m:(i // 2 + 1) * m] if i % 2 else zero_rows
                                         for i in range(CHUNK // m)], axis=0)
                else:
                    z = jnp.where(((row // m) % 2) == 1, qt, kk) * decay
                    g = gram(z.astype(BF16))
                scores = jnp.where(lvl == int(np.log2(m)), g, scores)
            near = qt * fg * pltpu.roll(kk, 1, 0)
            diag = qt * kk
            per_head = lambda x: jnp.where(head0_cols,
                                           jnp.sum(x[:, :HGRN_DK], axis=1, keepdims=True),
                                           jnp.sum(x[:, HGRN_DK:], axis=1, keepdims=True))
            scores = jnp.where(lvl == 0, per_head(near), scores)
            out.append((o_inter, jnp.where(lvl == -1, per_head(diag), scores)))
        return out

    def output_and_state(j, staged, scored):
        rows = slice(j * CHUNK, (j + 1) * CHUNK)
        outs = []
        for p in pairs:
            _, _, v, kk, sums = staged[p]
            o_inter, scores = scored[p]
            outs.append(o_inter + jnp.dot(scores.astype(BF16),
                                          block_diag(v[:, :HGRN_DK], v[:, HGRN_DK:], zero_c),
                                          preferred_element_type=F32))
            b = sums[0:CHUNK]
            b_last = b[CHUNK - 1:CHUNK, :]
            kd = (kk * jnp.exp2(b_last - b)).astype(BF16)
            dec = jnp.exp2(b_last)
            for hh, sl in ((2 * p, slice(0, HGRN_DK)), (2 * p + 1, slice(HGRN_DK, pair_w))):
                upd = lax.dot_general(kd[:, sl], v[:, sl], TN_DIMS, preferred_element_type=F32)
                dec_rows = jnp.broadcast_to(dec[:, sl], (HGRN_DK, HGRN_DK)).T
                st_ref[hh] = st_ref[hh] * dec_rows + upd
        o_all = jnp.concatenate(outs, axis=1)
        a_ref[rows, :] = (_rmsnorm(o_all, gn_ref[...]) * gate_ref[rows, :].astype(F32)).astype(BF16)

    staged = load_and_sum(0)
    for j in range(n_chunks):
        scored = scores_and_inter(staged)
        nxt = load_and_sum(j + 1) if j + 1 < n_chunks else None
        output_and_state(j, staged, scored)
        staged = nxt


def _hgrn_scan(q, f, v, gate, g_norm, bsz, seq):
    n_tok = q.shape[0]
    tc = SCAN_TILE
    steps = seq // tc
    toks = pl.BlockSpec((tc, D_MODEL), lambda b, c: (b * steps + c, 0))
    dmat = jnp.asarray(_decay_sum_matrix(), BF16)
    lvl = jnp.asarray(np.tile(_level_index(), (1, 2)))
    return pl.pallas_call(
        functools.partial(_hgrn_scan_kernel, n_chunks=tc // CHUNK),
        out_shape=jax.ShapeDtypeStruct((n_tok, D_MODEL), BF16),
        grid=(bsz, steps),
        in_specs=[toks, toks, toks, toks, _resident((1, D_MODEL)), _resident(dmat.shape),
                  _resident(lvl.shape)],
        out_specs=toks,
        scratch_shapes=[pltpu.VMEM((HGRN_HEADS, HGRN_DK, HGRN_DK), F32)],
        compiler_params=_params("parallel", "arbitrary"),
        name="hgrn_scan",
    )(q, f, v, gate, g_norm, dmat, lvl)


def kernel(x, positions, mix_norm, mlp_norm, final_norm, attn_w_qkv, attn_b_qkv, attn_sinks,
           attn_w_o, hgrn_w_in, hgrn_g_norm, hgrn_w_o, hgrn_lower_bounds, mlp_w_up, mlp_w_down):
    bsz, seq, _ = x.shape
    n_tok = bsz * seq
    xf = x.reshape(n_tok, D_MODEL)
    gain = lambda g: g.reshape(1, D_MODEL).astype(F32)

    qt, k, vt = _qkv_proj(xf, positions, gain(mix_norm[0]), attn_w_qkv[0].astype(BF16),
                          attn_b_qkv[0].reshape(1, QKV_DIM).astype(F32))
    a, w_o0, w_up0, w_down0, w_in, w_o1, w_up1, w_down1 = _attention(
        qt, k, vt, attn_sinks[0].astype(F32), bsz, seq,
        [(attn_w_o, 0), (mlp_w_up, 0), (mlp_w_down, 0), (hgrn_w_in, 0), (hgrn_w_o, 0),
         (mlp_w_up, 1), (mlp_w_down, 1)])
    xf, h1 = _out_mlp(a, xf, w_o0, gain(mlp_norm[0]), w_up0, w_down0, gain(mix_norm[1]), last=False)

    hq, hf, hv, gate = _hgrn_in(h1, w_in, hgrn_lower_bounds.astype(F32))
    a = _hgrn_scan(hq, hf, hv, gate, gain(hgrn_g_norm[0]), bsz, seq)
    (out,) = _out_mlp(a, xf, w_o1, gain(mlp_norm[1]), w_up1, w_down1, gain(final_norm), last=True)
    return out.reshape(bsz, seq, D_MODEL)
```

```python
import functools

import numpy as np
import jax
import jax.numpy as jnp
from jax import lax
from jax.experimental import pallas as pl
from jax.experimental.pallas import tpu as pltpu

D_MODEL = 1024
HEAD_DIM = 64
N_Q_HEADS = 16
N_KV_HEADS = 4
GROUP = 4
Q_DIM = N_Q_HEADS * HEAD_DIM
KV_DIM = N_KV_HEADS * HEAD_DIM
QKV_DIM = Q_DIM + 2 * KV_DIM
WINDOW = 128
BLOCK = 128
ROT_DIM = 16
ROT_HALF = ROT_DIM // 2
ROPE_THETA = 500000.0
NEG_INF = -1e30
HGRN_HEADS = 8
HGRN_DK = 128
CHUNK = 64
D_FF = 4 * D_MODEL
FF_CHUNK = 1024
AHEAD = 2
ITEM_HEADS = 4

ROW_TILE = 1024
SCAN_TILE = 256
SCAN_ROWS = 2
ATTN_Q_BLOCKS = 8
NORM_EPS = 1e-5
LOG2E = 1.4426950408889634

LANES = 128
BF16_SUBLANES = 16
VMEM_LIMIT = 56 * 1024 * 1024

BF16 = jnp.bfloat16
F32 = jnp.float32

NT_DIMS = (((1,), (1,)), ((), ()))
TN_DIMS = (((0,), (0,)), ((), ()))


def _rmsnorm(x, gain):
    ms = jnp.mean(x * x, axis=-1, keepdims=True)
    return x * lax.rsqrt(ms + NORM_EPS) * gain


def _params(*sem):
    return pltpu.CompilerParams(dimension_semantics=sem, vmem_limit_bytes=VMEM_LIMIT)


def _resident(shape):
    return pl.BlockSpec(shape, lambda *_: (0,) * len(shape))


def _cast_specs(weights, n_steps, step_index):
    in_specs, out_specs, shapes = [], [], []
    for w, layer in weights:
        _, n_rows, n_cols = w.shape
        rows = n_rows // n_steps
        in_specs.append(pl.BlockSpec((None, rows, n_cols),
                                     lambda *ids, layer=layer: (layer, step_index(*ids), 0)))
        out_specs.append(pl.BlockSpec((rows, n_cols), lambda *ids: (step_index(*ids), 0)))
        shapes.append(jax.ShapeDtypeStruct((n_rows, n_cols), BF16))
    return in_specs, out_specs, shapes


def _cast_slabs(in_refs, out_refs):
    for src, dst in zip(in_refs, out_refs, strict=True):
        dst[...] = src[...].astype(BF16)


def _qkv_kernel(x_ref, pos_ref, invf_ref, g_ref, w_ref, b_ref, qt_ref, k_ref, vt_ref):
    h = _rmsnorm(x_ref[...], g_ref[...]).astype(BF16)
    qkv = jnp.dot(h, w_ref[...], preferred_element_type=F32) + b_ref[...]
    ang = pos_ref[...].astype(F32) * invf_ref[...]
    cos_t = jnp.cos(ang)
    sin_t = jnp.sin(ang)

    def rotate_t(tt):
        parts = []
        for hd in range(LANES // HEAD_DIM):
            base = hd * HEAD_DIM
            x1 = tt[base:base + ROT_HALF]
            x2 = tt[base + ROT_HALF:base + ROT_DIM]
            parts += [x1 * cos_t - x2 * sin_t, x2 * cos_t + x1 * sin_t,
                      tt[base + ROT_DIM:base + HEAD_DIM]]
        return jnp.concatenate(parts, axis=0)

    scale = HEAD_DIM ** -0.5 * LOG2E
    for j in range(Q_DIM // LANES):
        t = qkv[:, j * LANES:(j + 1) * LANES] * scale
        qt_ref[j * LANES:(j + 1) * LANES, :] = rotate_t(t.T).astype(BF16)
    for j in range(KV_DIM // LANES):
        t = qkv[:, Q_DIM + j * LANES:Q_DIM + (j + 1) * LANES]
        k_ref[:, j * LANES:(j + 1) * LANES] = rotate_t(t.T).T.astype(BF16)
        t = qkv[:, Q_DIM + KV_DIM + j * LANES:Q_DIM + KV_DIM + (j + 1) * LANES]
        vt_ref[j * LANES:(j + 1) * LANES, :] = t.T.astype(BF16)


def _qkv_proj(x, positions, gain, w_qkv, b_qkv):
    n_tok = x.shape[0]
    tm = ROW_TILE
    row = lambda w: pl.BlockSpec((tm, w), lambda i: (i, 0))
    col = lambda h: pl.BlockSpec((h, tm), lambda i: (0, i))
    inv_freq = ROPE_THETA ** (-jnp.arange(0, ROT_DIM, 2, dtype=F32) / ROT_DIM)
    return pl.pallas_call(
        _qkv_kernel,
        out_shape=[jax.ShapeDtypeStruct((Q_DIM, n_tok), BF16),
                   jax.ShapeDtypeStruct((n_tok, KV_DIM), BF16),
                   jax.ShapeDtypeStruct((KV_DIM, n_tok), BF16)],
        grid=(n_tok // tm,),
        in_specs=[row(D_MODEL), col(1), _resident((ROT_HALF, 1)), _resident((1, D_MODEL)),
                  _resident((D_MODEL, QKV_DIM)), _resident((1, QKV_DIM))],
        out_specs=[col(Q_DIM), row(KV_DIM), col(KV_DIM)],
        compiler_params=_params("parallel"),
        name="qkv_rope",
    )(x, positions.reshape(1, n_tok), inv_freq.reshape(ROT_HALF, 1), gain, w_qkv, b_qkv)


def _attn_kernel(sink_ref, qt_ref, kp_ref, kc_ref, vtp_ref, vtc_ref, *rest, q_blocks, n_cast):
    cast_in, o_ref, cast_out, bias_ref = rest[:n_cast], rest[n_cast], rest[n_cast + 1:-1], rest[-1]
    _cast_slabs(cast_in, cast_out)
    n = pl.program_id(1)
    band = 2 * BLOCK
    width = ITEM_HEADS * BLOCK
    kj = lax.broadcasted_iota(jnp.int32, (band, width), 0)
    qi = lax.broadcasted_iota(jnp.int32, (band, width), 1) % BLOCK
    delta = qi + BLOCK - kj
    in_window = (delta >= 0) & (delta < WINDOW)
    bias_ref[0] = jnp.where(in_window & ((kj >= BLOCK) | (n > 0)), 0.0, NEG_INF)
    bias_ref[1] = jnp.where(in_window, 0.0, NEG_INF)
    head_of_lane = lax.broadcasted_iota(jnp.int32, (1, width), 1) // BLOCK
    zero_q = jnp.zeros((HEAD_DIM, width), BF16)
    ones_rows = jnp.ones((BF16_SUBLANES, band), BF16)

    def scores(j, kvh, part):
        tile = slice((kvh // 2) * LANES, (kvh // 2 + 1) * LANES)
        if j == 0:
            kband = jnp.concatenate([kp_ref[:, tile], kc_ref[:BLOCK, tile]], axis=0)
        else:
            kband = kc_ref[(j - 1) * BLOCK:(j + 1) * BLOCK, tile]
        qc = slice(j * BLOCK, (j + 1) * BLOCK)
        heads = [GROUP * kvh + part * ITEM_HEADS + g for g in range(ITEM_HEADS)]
        qt4 = jnp.concatenate([qt_ref[hd * HEAD_DIM:(hd + 1) * HEAD_DIM, qc] for hd in heads], axis=1)
        rhs = jnp.concatenate([qt4, zero_q] if kvh % 2 == 0 else [zero_q, qt4], axis=0)
        s = jnp.dot(kband, rhs, preferred_element_type=F32)
        return s + bias_ref[0 if j == 0 else 1]

    def softmax_pv(j, kvh, part, s):
        hs = slice(kvh * HEAD_DIM, (kvh + 1) * HEAD_DIM)
        if j == 0:
            vt = jnp.concatenate([vtp_ref[hs, :], vtc_ref[hs, :BLOCK]], axis=1)
        else:
            vt = vtc_ref[hs, (j - 1) * BLOCK:(j + 1) * BLOCK]
        sink = jnp.zeros((1, width), F32)
        for g in range(ITEM_HEADS):
            sink = jnp.where(head_of_lane == g,
                             sink_ref[GROUP * kvh + part * ITEM_HEADS + g] * LOG2E, sink)
        m = jnp.maximum(jnp.max(s, axis=0, keepdims=True), sink)
        e = jnp.exp2(s - m).astype(BF16)
        pv = jnp.dot(jnp.concatenate([vt, ones_rows], axis=0), e,
                     preferred_element_type=F32)
        den = pv[HEAD_DIM:HEAD_DIM + 1, :] + jnp.exp2(sink - m)
        out_t = pv[:HEAD_DIM, :] * (1.0 / den)
        qr = slice(j * BLOCK, (j + 1) * BLOCK)
        tiles_per_item = ITEM_HEADS * HEAD_DIM // LANES
        for t in range(tiles_per_item):
            pair = jnp.concatenate([out_t[:, (2 * t) * BLOCK:(2 * t + 1) * BLOCK],
                                    out_t[:, (2 * t + 1) * BLOCK:(2 * t + 2) * BLOCK]], axis=0)
            tile = GROUP * HEAD_DIM // LANES * kvh + part * tiles_per_item + t
            o_ref[qr, tile * LANES:(tile + 1) * LANES] = pair.T.astype(BF16)

    items = [(j, kvh, part) for j in range(q_blocks) for kvh in range(N_KV_HEADS)
             for part in range(GROUP // ITEM_HEADS)]
    pending = [scores(*item) for item in items[:AHEAD]]
    for i, item in enumerate(items):
        if i + AHEAD < len(items):
            pending.append(scores(*items[i + AHEAD]))
        softmax_pv(*item, pending.pop(0))


def _attention(qt, k, vt, sinks, bsz, seq, cast_weights):
    q_blocks = ATTN_Q_BLOCKS
    tq = q_blocks * BLOCK
    steps = seq // tq
    cast_in, cast_out, cast_shapes = _cast_specs(cast_weights, bsz * steps, lambda b, n: b * steps + n)
    nb = seq // BLOCK
    cur = lambda b, n: (b * steps + n, 0)
    prev = lambda b, n: (b * nb + jnp.maximum(n * q_blocks - 1, 0), 0)
    cur_t = lambda b, n: (0, b * steps + n)
    prev_t = lambda b, n: (0, b * nb + jnp.maximum(n * q_blocks - 1, 0))
    return pl.pallas_call(
        functools.partial(_attn_kernel, q_blocks=q_blocks, n_cast=len(cast_weights)),
        out_shape=[jax.ShapeDtypeStruct((bsz * seq, Q_DIM), BF16)] + cast_shapes,
        grid=(bsz, steps),
        in_specs=[pl.BlockSpec(memory_space=pltpu.SMEM),
                  pl.BlockSpec((Q_DIM, tq), cur_t),
                  pl.BlockSpec((BLOCK, KV_DIM), prev), pl.BlockSpec((tq, KV_DIM), cur),
                  pl.BlockSpec((KV_DIM, BLOCK), prev_t), pl.BlockSpec((KV_DIM, tq), cur_t)] + cast_in,
        out_specs=[pl.BlockSpec((tq, Q_DIM), cur)] + cast_out,
        scratch_shapes=[pltpu.VMEM((2, 2 * BLOCK, ITEM_HEADS * BLOCK), F32)],
        compiler_params=_params("parallel", "parallel"),
        name="swa_attention",
    )(sinks, qt, k, k, vt, vt, *[w for w, _ in cast_weights])


def _out_mlp_kernel(a_ref, x_ref, wo_ref, g_ref, wu_ref, wd_ref, gn_ref, *out_refs, last):
    x1 = x_ref[...] + jnp.dot(a_ref[...], wo_ref[...], preferred_element_type=F32)
    h = _rmsnorm(x1, g_ref[...]).astype(BF16)
    x2 = x1
    for c in range(D_FF // FF_CHUNK):
        cols = slice(c * FF_CHUNK, (c + 1) * FF_CHUNK)
        u = jnp.maximum(jnp.dot(h, wu_ref[:, cols], preferred_element_type=F32), 0.0)
        x2 = x2 + jnp.dot((u * u).astype(BF16), wd_ref[cols, :], preferred_element_type=F32)
    normed = _rmsnorm(x2, gn_ref[...])
    if last:
        out_refs[0][...] = normed
    else:
        out_refs[0][...] = x2
        out_refs[1][...] = normed.astype(BF16)


def _out_mlp(a, x, w_o, gain, w_up, w_down, next_gain, last):
    n_tok = x.shape[0]
    tm = ROW_TILE
    row = pl.BlockSpec((tm, D_MODEL), lambda i: (i, 0))
    once = pl.Buffered(1)
    wspec = lambda shape: pl.BlockSpec(shape, lambda i: (0, 0), pipeline_mode=once)
    stream = jax.ShapeDtypeStruct((n_tok, D_MODEL), F32)
    return pl.pallas_call(
        functools.partial(_out_mlp_kernel, last=last),
        out_shape=[stream] if last else [stream, jax.ShapeDtypeStruct((n_tok, D_MODEL), BF16)],
        grid=(n_tok // tm,),
        in_specs=[row, row, wspec((D_MODEL, D_MODEL)), wspec((1, D_MODEL)),
                  wspec((D_MODEL, D_FF)), wspec((D_FF, D_MODEL)), wspec((1, D_MODEL))],
        out_specs=[row] if last else [row, row],
        compiler_params=_params("parallel"),
        name="out_proj_mlp",
    )(a, x, w_o, gain, w_up, w_down, next_gain)


def _hgrn_in_kernel(h_ref, w_ref, lbp_ref, q_ref, f_ref, v_ref, gate_ref):
    h = h_ref[...]
    lbp = lbp_ref[...]
    e = jnp.exp(lbp - jnp.max(lbp, axis=0, keepdims=True))
    sm = e / jnp.sum(e, axis=0, keepdims=True)
    lb = (sm[0:1] + sm[1:2]) - sm[0:1]

    def proj(c):
        return jnp.dot(h, w_ref[:, c * D_MODEL:(c + 1) * D_MODEL], preferred_element_type=F32)

    q = proj(0)
    f = proj(1)
    q_ref[...] = (q * jax.nn.sigmoid(q)).astype(BF16)
    g = proj(3)
    f_ref[...] = lb + (1.0 - lb) * jax.nn.sigmoid(f)
    i = proj(2)
    gate_ref[...] = (g * jax.nn.sigmoid(g)).astype(BF16)
    v_ref[...] = i.astype(BF16)


def _hgrn_in(h, w_in, lb_params):
    n_tok = h.shape[0]
    tm = ROW_TILE
    row = pl.BlockSpec((tm, D_MODEL), lambda i: (i, 0))
    once = pl.Buffered(1)
    out = lambda dtype: jax.ShapeDtypeStruct((n_tok, D_MODEL), dtype)
    return pl.pallas_call(
        _hgrn_in_kernel,
        out_shape=[out(BF16), out(F32), out(BF16), out(BF16)],
        grid=(n_tok // tm,),
        in_specs=[row,
                  pl.BlockSpec((D_MODEL, 4 * D_MODEL), lambda i: (0, 0), pipeline_mode=once),
                  _resident((2, D_MODEL))],
        out_specs=[row, row, row, row],
        compiler_params=_params("parallel"),
        name="hgrn_in_proj",
    )(h, w_in, lb_params)


LEVELS = (32, 16, 8, 4, 2)
SPLIT = 2


def _decay_sum_matrix():
    t = np.arange(CHUNK)[:, None]
    u = np.arange(CHUNK)[None, :]
    mats = [u <= t]
    for m in LEVELS:
        blk = t // m
        odd = (blk % 2) == 1
        q_side = (u >= blk * m) & (u <= t)
        k_side = (u > t) & (u < (blk + 1) * m)
        mats.append(np.where(odd, q_side, k_side))
    d = np.concatenate(mats, axis=0).astype(np.float32)
    return np.concatenate([d] * SPLIT, axis=1)


def _level_index():
    t = np.arange(CHUNK)[:, None]
    s = np.arange(CHUNK)[None, :]
    x = t ^ s
    lvl = np.floor(np.log2(np.maximum(x, 1))).astype(np.int32)
    lvl = np.where(t == s, -1, lvl)
    return np.where(t < s, -2, lvl).astype(np.int32)


def _hgrn_scan_kernel(q_ref, f_ref, v_ref, gate_ref, gn_ref, dmat_ref, lvl_ref, a_ref, st_ref,
                      *, n_chunks):
    @pl.when(pl.program_id(1) == 0)
    def _():
        st_ref[...] = jnp.zeros_like(st_ref)

    pair_w = 2 * HGRN_DK
    row = lax.broadcasted_iota(jnp.int32, (CHUNK, pair_w), 0)
    lvl = lvl_ref[...]
    head0_cols = lax.broadcasted_iota(jnp.int32, (CHUNK, 2 * CHUNK), 1) < CHUNK
    dmat = dmat_ref[...]
    zero_c = jnp.zeros((CHUNK, HGRN_DK), BF16)
    zero_s = jnp.zeros((HGRN_DK, HGRN_DK), BF16)

    def block_diag(a0, a1, zero):
        return jnp.concatenate([jnp.concatenate([a0, zero], axis=1),
                                jnp.concatenate([zero, a1], axis=1)], axis=0)

    pairs = range(SCAN_ROWS * HGRN_HEADS // 2)

    def gram(z):
        zb = block_diag(z[:, :HGRN_DK], z[:, HGRN_DK:], zero_c)
        return lax.dot_general(z, zb, NT_DIMS, preferred_element_type=F32)

    def load_and_sum(j):
        rows = slice(j * CHUNK, (j + 1) * CHUNK)
        out = []
        for p in pairs:
            r, hp = divmod(p, HGRN_HEADS // 2)
            cols = slice(hp * pair_w, (hp + 1) * pair_w)
            qt, fg, v = q_ref[r, rows, cols].astype(F32), f_ref[r, rows, cols], v_ref[r, rows, cols]
            lf = jnp.log2(fg)
            pieces, rest = [], lf
            for _ in range(SPLIT):
                piece = rest.astype(BF16)
                pieces.append(piece)
                rest = rest - piece.astype(F32)
            sums = jnp.dot(dmat, jnp.concatenate(pieces, axis=0), preferred_element_type=F32)
            out.append((qt, fg, v, 1.0 - fg, sums))
        return out

    def scores_and_inter(staged):
        out = []
        for p in pairs:
            qt, fg, _, kk, sums = staged[p]
            qe = (qt * jnp.exp2(sums[0:CHUNK])).astype(BF16)
            st_bd = block_diag(st_ref[2 * p].astype(BF16), st_ref[2 * p + 1].astype(BF16), zero_s)
            o_inter = jnp.dot(qe, st_bd, preferred_element_type=F32)
            scores = jnp.zeros((CHUNK, 2 * CHUNK), F32)
            for li, m in enumerate(LEVELS):
                decay = jnp.exp2(sums[(1 + li) * CHUNK:(2 + li) * CHUNK])
                if m >= 8:
                    blocks = [slice(i * m, (i + 1) * m) for i in range(CHUNK // m)]
                    zq = jnp.concatenate([qt[b] * decay[b] for b in blocks[1::2]], axis=0)
                    zk = jnp.concatenate([kk[b] * decay[b] if i % 2 == 0 else jnp.zeros((m, pair_w), F32)
                                          for i, b in enumerate(blocks)], axis=0).astype(BF16)
                    part = lax.dot_general(zq.astype(BF16),
                                           block_diag(zk[:, :HGRN_DK], zk[:, HGRN_DK:], zero_c),
                                           NT_DIMS, preferred_element_type=F32)
                    zero_rows = jnp.zeros((m, 2 * CHUNK), F32)
                    g = jnp.concatenate([part[(i // 2) * m:(i // 2 + 1) * m] if i % 2 else zero_rows
                                         for i in range(CHUNK // m)], axis=0)
                else:
                    z = jnp.where(((row // m) % 2) == 1, qt, kk) * decay
                    g = gram(z.astype(BF16))
                scores = jnp.where(lvl == int(np.log2(m)), g, scores)
            near = qt * fg * pltpu.roll(kk, 1, 0)
            diag = qt * kk
            per_head = lambda x: jnp.where(head0_cols,
                                           jnp.sum(x[:, :HGRN_DK], axis=1, keepdims=True),
                                           jnp.sum(x[:, HGRN_DK:], axis=1, keepdims=True))
            scores = jnp.where(lvl == 0, per_head(near), scores)
            out.append((o_inter, jnp.where(lvl == -1, per_head(diag), scores)))
        return out

    def output_and_state(j, staged, scored):
        rows = slice(j * CHUNK, (j + 1) * CHUNK)
        outs = []
        for p in pairs:
            _, _, v, kk, sums = staged[p]
            o_inter, scores = scored[p]
            outs.append(o_inter + jnp.dot(scores.astype(BF16),
                                          block_diag(v[:, :HGRN_DK], v[:, HGRN_DK:], zero_c),
                                          preferred_element_type=F32))
            b = sums[0:CHUNK]
            b_last = b[CHUNK - 1:CHUNK, :]
            kd = (kk * jnp.exp2(b_last - b)).astype(BF16)
            dec = jnp.exp2(b_last)
            for hh, sl in ((2 * p, slice(0, HGRN_DK)), (2 * p + 1, slice(HGRN_DK, pair_w))):
                upd = lax.dot_general(kd[:, sl], v[:, sl], TN_DIMS, preferred_element_type=F32)
                dec_rows = jnp.broadcast_to(dec[:, sl], (HGRN_DK, HGRN_DK)).T
                st_ref[hh] = st_ref[hh] * dec_rows + upd
        per_row = HGRN_HEADS // 2
        for r in range(SCAN_ROWS):
            o_all = jnp.concatenate(outs[r * per_row:(r + 1) * per_row], axis=1)
            a_ref[r, rows, :] = (_rmsnorm(o_all, gn_ref[...])
                                 * gate_ref[r, rows, :].astype(F32)).astype(BF16)

    staged = load_and_sum(0)
    for j in range(n_chunks):
        scored = scores_and_inter(staged)
        nxt = load_and_sum(j + 1) if j + 1 < n_chunks else None
        output_and_state(j, staged, scored)
        staged = nxt


def _hgrn_scan(q, f, v, gate, g_norm, bsz, seq):
    tc = SCAN_TILE
    steps = seq // tc
    toks = pl.BlockSpec((SCAN_ROWS, tc, D_MODEL), lambda b, c: (b, c, 0))
    dmat = jnp.asarray(_decay_sum_matrix(), BF16)
    lvl = jnp.asarray(np.tile(_level_index(), (1, 2)))
    by_row = lambda t: t.reshape(bsz, seq, D_MODEL)
    out = pl.pallas_call(
        functools.partial(_hgrn_scan_kernel, n_chunks=tc // CHUNK),
        out_shape=jax.ShapeDtypeStruct((bsz, seq, D_MODEL), BF16),
        grid=(bsz // SCAN_ROWS, steps),
        in_specs=[toks, toks, toks, toks, _resident((1, D_MODEL)), _resident(dmat.shape),
                  _resident(lvl.shape)],
        out_specs=toks,
        scratch_shapes=[pltpu.VMEM((SCAN_ROWS * HGRN_HEADS, HGRN_DK, HGRN_DK), F32)],
        compiler_params=_params("parallel", "arbitrary"),
        name="hgrn_scan",
    )(by_row(q), by_row(f), by_row(v), by_row(gate), g_norm, dmat, lvl)
    return out.reshape(bsz * seq, D_MODEL)


def kernel(x, positions, mix_norm, mlp_norm, final_norm, attn_w_qkv, attn_b_qkv, attn_sinks,
           attn_w_o, hgrn_w_in, hgrn_g_norm, hgrn_w_o, hgrn_lower_bounds, mlp_w_up, mlp_w_down):
    bsz, seq, _ = x.shape
    n_tok = bsz * seq
    xf = x.reshape(n_tok, D_MODEL)
    gain = lambda g: g.reshape(1, D_MODEL).astype(F32)

    qt, k, vt = _qkv_proj(xf, positions, gain(mix_norm[0]), attn_w_qkv[0].astype(BF16),
                          attn_b_qkv[0].reshape(1, QKV_DIM).astype(F32))
    a, w_o0, w_up0, w_down0, w_in, w_o1, w_up1, w_down1 = _attention(
        qt, k, vt, attn_sinks[0].astype(F32), bsz, seq,
        [(attn_w_o, 0), (mlp_w_up, 0), (mlp_w_down, 0), (hgrn_w_in, 0), (hgrn_w_o, 0),
         (mlp_w_up, 1), (mlp_w_down, 1)])
    xf, h1 = _out_mlp(a, xf, w_o0, gain(mlp_norm[0]), w_up0, w_down0, gain(mix_norm[1]), last=False)

    hq, hf, hv, gate = _hgrn_in(h1, w_in, hgrn_lower_bounds.astype(F32))
    a = _hgrn_scan(hq, hf, hv, gate, gain(hgrn_g_norm[0]), bsz, seq)
    (out,) = _out_mlp(a, xf, w_o1, gain(mlp_norm[1]), w_up1, w_down1, gain(final_norm), last=True)
    return out.reshape(bsz, seq, D_MODEL)
```

```python
import functools

import numpy as np
import jax
import jax.numpy as jnp
from jax import lax
from jax.experimental import pallas as pl
from jax.experimental.pallas import tpu as pltpu

D_MODEL = 1024
HEAD_DIM = 64
N_Q_HEADS = 16
N_KV_HEADS = 4
GROUP = 4
Q_DIM = N_Q_HEADS * HEAD_DIM
KV_DIM = N_KV_HEADS * HEAD_DIM
QKV_DIM = Q_DIM + 2 * KV_DIM
WINDOW = 128
BLOCK = 128
ROT_DIM = 16
ROT_HALF = ROT_DIM // 2
ROPE_THETA = 500000.0
NEG_INF = -1e30
HGRN_HEADS = 8
HGRN_DK = 128
CHUNK = 64
D_FF = 4 * D_MODEL
FF_CHUNK = 1024
AHEAD = 2

ROW_TILE = 1024
SCAN_TILE = 512
ATTN_Q_BLOCKS = 8
NORM_EPS = 1e-5
LOG2E = 1.4426950408889634

LANES = 128
BF16_SUBLANES = 16
VMEM_LIMIT = 56 * 1024 * 1024

BF16 = jnp.bfloat16
F32 = jnp.float32

NT_DIMS = (((1,), (1,)), ((), ()))
TN_DIMS = (((0,), (0,)), ((), ()))


def _rmsnorm(x, gain):
    ms = jnp.mean(x * x, axis=-1, keepdims=True)
    return x * lax.rsqrt(ms + NORM_EPS) * gain


def _params(*sem):
    return pltpu.CompilerParams(dimension_semantics=sem, vmem_limit_bytes=VMEM_LIMIT)


def _resident(shape):
    return pl.BlockSpec(shape, lambda *_: (0,) * len(shape))


def _cast_specs(weights, n_steps, step_index):
    in_specs, out_specs, shapes = [], [], []
    for w, layer in weights:
        _, n_rows, n_cols = w.shape
        rows = n_rows // n_steps
        in_specs.append(pl.BlockSpec((None, rows, n_cols),
                                     lambda *ids, layer=layer: (layer, step_index(*ids), 0)))
        out_specs.append(pl.BlockSpec((rows, n_cols), lambda *ids: (step_index(*ids), 0)))
        shapes.append(jax.ShapeDtypeStruct((n_rows, n_cols), BF16))
    return in_specs, out_specs, shapes


def _cast_slabs(in_refs, out_refs):
    for src, dst in zip(in_refs, out_refs, strict=True):
        dst[...] = src[...].astype(BF16)


def _qkv_kernel(x_ref, pos_ref, invf_ref, g_ref, w_ref, b_ref, qt_ref, k_ref, vt_ref):
    h = _rmsnorm(x_ref[...], g_ref[...]).astype(BF16)
    qkv = jnp.dot(h, w_ref[...], preferred_element_type=F32) + b_ref[...]
    ang = pos_ref[...].astype(F32) * invf_ref[...]
    cos_t = jnp.cos(ang)
    sin_t = jnp.sin(ang)

    def rotate_t(tt):
        parts = []
        for hd in range(LANES // HEAD_DIM):
            base = hd * HEAD_DIM
            x1 = tt[base:base + ROT_HALF]
            x2 = tt[base + ROT_HALF:base + ROT_DIM]
            parts += [x1 * cos_t - x2 * sin_t, x2 * cos_t + x1 * sin_t,
                      tt[base + ROT_DIM:base + HEAD_DIM]]
        return jnp.concatenate(parts, axis=0)

    scale = HEAD_DIM ** -0.5 * LOG2E
    for j in range(Q_DIM // LANES):
        t = qkv[:, j * LANES:(j + 1) * LANES] * scale
        qt_ref[j * LANES:(j + 1) * LANES, :] = rotate_t(t.T).astype(BF16)
    for j in range(KV_DIM // LANES):
        t = qkv[:, Q_DIM + j * LANES:Q_DIM + (j + 1) * LANES]
        k_ref[:, j * LANES:(j + 1) * LANES] = rotate_t(t.T).T.astype(BF16)
        t = qkv[:, Q_DIM + KV_DIM + j * LANES:Q_DIM + KV_DIM + (j + 1) * LANES]
        vt_ref[j * LANES:(j + 1) * LANES, :] = t.T.astype(BF16)


def _qkv_proj(x, positions, gain, w_qkv, b_qkv):
    n_tok = x.shape[0]
    tm = ROW_TILE
    row = lambda w: pl.BlockSpec((tm, w), lambda i: (i, 0))
    col = lambda h: pl.BlockSpec((h, tm), lambda i: (0, i))
    inv_freq = ROPE_THETA ** (-jnp.arange(0, ROT_DIM, 2, dtype=F32) / ROT_DIM)
    return pl.pallas_call(
        _qkv_kernel,
        out_shape=[jax.ShapeDtypeStruct((Q_DIM, n_tok), BF16),
                   jax.ShapeDtypeStruct((n_tok, KV_DIM), BF16),
                   jax.ShapeDtypeStruct((KV_DIM, n_tok), BF16)],
        grid=(n_tok // tm,),
        in_specs=[row(D_MODEL), col(1), _resident((ROT_HALF, 1)), _resident((1, D_MODEL)),
                  _resident((D_MODEL, QKV_DIM)), _resident((1, QKV_DIM))],
        out_specs=[col(Q_DIM), row(KV_DIM), col(KV_DIM)],
        compiler_params=_params("parallel"),
        name="qkv_rope",
    )(x, positions.reshape(1, n_tok), inv_freq.reshape(ROT_HALF, 1), gain, w_qkv, b_qkv)


def _attn_kernel(sink_ref, qt_ref, kp_ref, kc_ref, vtp_ref, vtc_ref, *rest, q_blocks, n_cast):
    cast_in, o_ref, cast_out, bias_ref = rest[:n_cast], rest[n_cast], rest[n_cast + 1:-1], rest[-1]
    _cast_slabs(cast_in, cast_out)
    n = pl.program_id(1)
    band = 2 * BLOCK
    width = GROUP * BLOCK
    kj = lax.broadcasted_iota(jnp.int32, (band, width), 0)
    qi = lax.broadcasted_iota(jnp.int32, (band, width), 1) % BLOCK
    delta = qi + BLOCK - kj
    in_window = (delta >= 0) & (delta < WINDOW)
    bias_ref[0] = jnp.where(in_window & ((kj >= BLOCK) | (n > 0)), 0.0, NEG_INF)
    bias_ref[1] = jnp.where(in_window, 0.0, NEG_INF)
    head_of_lane = lax.broadcasted_iota(jnp.int32, (1, width), 1) // BLOCK
    zero_q = jnp.zeros((HEAD_DIM, width), BF16)
    ones_rows = jnp.ones((BF16_SUBLANES, band), BF16)

    def scores(j, kvh):
        tile = slice((kvh // 2) * LANES, (kvh // 2 + 1) * LANES)
        if j == 0:
            kband = jnp.concatenate([kp_ref[:, tile], kc_ref[:BLOCK, tile]], axis=0)
        else:
            kband = kc_ref[(j - 1) * BLOCK:(j + 1) * BLOCK, tile]
        qc = slice(j * BLOCK, (j + 1) * BLOCK)
        qt4 = jnp.concatenate(
            [qt_ref[(GROUP * kvh + g) * HEAD_DIM:(GROUP * kvh + g + 1) * HEAD_DIM, qc]
             for g in range(GROUP)], axis=1)
        rhs = jnp.concatenate([qt4, zero_q] if kvh % 2 == 0 else [zero_q, qt4], axis=0)
        s = jnp.dot(kband, rhs, preferred_element_type=F32)
        return s + bias_ref[0 if j == 0 else 1]

    def softmax_pv(j, kvh, s):
        hs = slice(kvh * HEAD_DIM, (kvh + 1) * HEAD_DIM)
        if j == 0:
            vt = jnp.concatenate([vtp_ref[hs, :], vtc_ref[hs, :BLOCK]], axis=1)
        else:
            vt = vtc_ref[hs, (j - 1) * BLOCK:(j + 1) * BLOCK]
        sink = jnp.zeros((1, width), F32)
        for g in range(GROUP):
            sink = jnp.where(head_of_lane == g, sink_ref[GROUP * kvh + g] * LOG2E, sink)
        m = jnp.maximum(jnp.max(s, axis=0, keepdims=True), sink)
        e = jnp.exp2(s - m).astype(BF16)
        pv = jnp.dot(jnp.concatenate([vt, ones_rows], axis=0), e,
                     preferred_element_type=F32)
        den = pv[HEAD_DIM:HEAD_DIM + 1, :] + jnp.exp2(sink - m)
        out_t = pv[:HEAD_DIM, :] * (1.0 / den)
        qr = slice(j * BLOCK, (j + 1) * BLOCK)
        for t in range(GROUP * HEAD_DIM // LANES):
            pair = jnp.concatenate([out_t[:, (2 * t) * BLOCK:(2 * t + 1) * BLOCK],
                                    out_t[:, (2 * t + 1) * BLOCK:(2 * t + 2) * BLOCK]], axis=0)
            o_ref[qr, (2 * kvh + t) * LANES:(2 * kvh + t + 1) * LANES] = pair.T.astype(BF16)

    items = [(j, kvh) for j in range(q_blocks) for kvh in range(N_KV_HEADS)]
    pending = [scores(*item) for item in items[:AHEAD]]
    for i, item in enumerate(items):
        if i + AHEAD < len(items):
            pending.append(scores(*items[i + AHEAD]))
        softmax_pv(*item, pending.pop(0))


def _attention(qt, k, vt, sinks, bsz, seq, cast_weights):
    q_blocks = ATTN_Q_BLOCKS
    tq = q_blocks * BLOCK
    steps = seq // tq
    cast_in, cast_out, cast_shapes = _cast_specs(cast_weights, bsz * steps, lambda b, n: b * steps + n)
    nb = seq // BLOCK
    cur = lambda b, n: (b * steps + n, 0)
    prev = lambda b, n: (b * nb + jnp.maximum(n * q_blocks - 1, 0), 0)
    cur_t = lambda b, n: (0, b * steps + n)
    prev_t = lambda b, n: (0, b * nb + jnp.maximum(n * q_blocks - 1, 0))
    return pl.pallas_call(
        functools.partial(_attn_kernel, q_blocks=q_blocks, n_cast=len(cast_weights)),
        out_shape=[jax.ShapeDtypeStruct((bsz * seq, Q_DIM), BF16)] + cast_shapes,
        grid=(bsz, steps),
        in_specs=[pl.BlockSpec(memory_space=pltpu.SMEM),
                  pl.BlockSpec((Q_DIM, tq), cur_t),
                  pl.BlockSpec((BLOCK, KV_DIM), prev), pl.BlockSpec((tq, KV_DIM), cur),
                  pl.BlockSpec((KV_DIM, BLOCK), prev_t), pl.BlockSpec((KV_DIM, tq), cur_t)] + cast_in,
        out_specs=[pl.BlockSpec((tq, Q_DIM), cur)] + cast_out,
        scratch_shapes=[pltpu.VMEM((2, 2 * BLOCK, GROUP * BLOCK), F32)],
        compiler_params=_params("parallel", "parallel"),
        name="swa_attention",
    )(sinks, qt, k, k, vt, vt, *[w for w, _ in cast_weights])


def _out_mlp_kernel(a_ref, x_ref, wo_ref, g_ref, wu_ref, wd_ref, gn_ref, *out_refs, last):
    x1 = x_ref[...] + jnp.dot(a_ref[...], wo_ref[...], preferred_element_type=F32)
    h = _rmsnorm(x1, g_ref[...]).astype(BF16)
    x2 = x1
    for c in range(D_FF // FF_CHUNK):
        cols = slice(c * FF_CHUNK, (c + 1) * FF_CHUNK)
        u = jnp.maximum(jnp.dot(h, wu_ref[:, cols], preferred_element_type=F32), 0.0)
        x2 = x2 + jnp.dot((u * u).astype(BF16), wd_ref[cols, :], preferred_element_type=F32)
    normed = _rmsnorm(x2, gn_ref[...])
    if last:
        out_refs[0][...] = normed
    else:
        out_refs[0][...] = x2
        out_refs[1][...] = normed.astype(BF16)


def _out_mlp(a, x, w_o, gain, w_up, w_down, next_gain, last):
    n_tok = x.shape[0]
    tm = ROW_TILE
    row = pl.BlockSpec((tm, D_MODEL), lambda i: (i, 0))
    once = pl.Buffered(1)
    wspec = lambda shape: pl.BlockSpec(shape, lambda i: (0, 0), pipeline_mode=once)
    stream = jax.ShapeDtypeStruct((n_tok, D_MODEL), F32)
    return pl.pallas_call(
        functools.partial(_out_mlp_kernel, last=last),
        out_shape=[stream] if last else [stream, jax.ShapeDtypeStruct((n_tok, D_MODEL), BF16)],
        grid=(n_tok // tm,),
        in_specs=[row, row, wspec((D_MODEL, D_MODEL)), wspec((1, D_MODEL)),
                  wspec((D_MODEL, D_FF)), wspec((D_FF, D_MODEL)), wspec((1, D_MODEL))],
        out_specs=[row] if last else [row, row],
        compiler_params=_params("parallel"),
        name="out_proj_mlp",
    )(a, x, w_o, gain, w_up, w_down, next_gain)


def _hgrn_in_kernel(h_ref, w_ref, lbp_ref, q_ref, f_ref, v_ref, gate_ref):
    h = h_ref[...]
    lbp = lbp_ref[...]
    e = jnp.exp(lbp - jnp.max(lbp, axis=0, keepdims=True))
    sm = e / jnp.sum(e, axis=0, keepdims=True)
    lb = (sm[0:1] + sm[1:2]) - sm[0:1]

    def proj(c):
        return jnp.dot(h, w_ref[:, c * D_MODEL:(c + 1) * D_MODEL], preferred_element_type=F32)

    q = proj(0)
    f = proj(1)
    q_ref[...] = (q * jax.nn.sigmoid(q)).astype(BF16)
    g = proj(3)
    f_ref[...] = lb + (1.0 - lb) * jax.nn.sigmoid(f)
    i = proj(2)
    gate_ref[...] = (g * jax.nn.sigmoid(g)).astype(BF16)
    v_ref[...] = i.astype(BF16)


def _hgrn_in(h, w_in, lb_params):
    n_tok = h.shape[0]
    tm = ROW_TILE
    row = pl.BlockSpec((tm, D_MODEL), lambda i: (i, 0))
    once = pl.Buffered(1)
    out = lambda dtype: jax.ShapeDtypeStruct((n_tok, D_MODEL), dtype)
    return pl.pallas_call(
        _hgrn_in_kernel,
        out_shape=[out(BF16), out(F32), out(BF16), out(BF16)],
        grid=(n_tok // tm,),
        in_specs=[row,
                  pl.BlockSpec((D_MODEL, 4 * D_MODEL), lambda i: (0, 0), pipeline_mode=once),
                  _resident((2, D_MODEL))],
        out_specs=[row, row, row, row],
        compiler_params=_params("parallel"),
        name="hgrn_in_proj",
    )(h, w_in, lb_params)


LEVELS = (32, 16, 8, 4, 2)
SPLIT = 2


def _decay_sum_matrix():
    t = np.arange(CHUNK)[:, None]
    u = np.arange(CHUNK)[None, :]
    mats = [u <= t]
    for m in LEVELS:
        blk = t // m
        odd = (blk % 2) == 1
        q_side = (u >= blk * m) & (u <= t)
        k_side = (u > t) & (u < (blk + 1) * m)
        mats.append(np.where(odd, q_side, k_side))
    d = np.concatenate(mats, axis=0).astype(np.float32)
    return np.concatenate([d] * SPLIT, axis=1)


def _level_index():
    t = np.arange(CHUNK)[:, None]
    s = np.arange(CHUNK)[None, :]
    x = t ^ s
    lvl = np.floor(np.log2(np.maximum(x, 1))).astype(np.int32)
    lvl = np.where(t == s, -1, lvl)
    return np.where(t < s, -2, lvl).astype(np.int32)


def _hgrn_scan_kernel(q_ref, f_ref, v_ref, gate_ref, gn_ref, dmat_ref, lvl_ref, a_ref, st_ref,
                      *, n_chunks):
    @pl.when(pl.program_id(1) == 0)
    def _():
        st_ref[...] = jnp.zeros_like(st_ref)

    pair_w = 2 * HGRN_DK
    row = lax.broadcasted_iota(jnp.int32, (CHUNK, pair_w), 0)
    lvl = lvl_ref[...]
    head0_cols = lax.broadcasted_iota(jnp.int32, (CHUNK, 2 * CHUNK), 1) < CHUNK
    dmat = dmat_ref[...]
    zero_c = jnp.zeros((CHUNK, HGRN_DK), BF16)
    zero_s = jnp.zeros((HGRN_DK, HGRN_DK), BF16)

    def block_diag(a0, a1, zero):
        return jnp.concatenate([jnp.concatenate([a0, zero], axis=1),
                                jnp.concatenate([zero, a1], axis=1)], axis=0)

    pairs = range(HGRN_HEADS // 2)

    def gram(z):
        zb = block_diag(z[:, :HGRN_DK], z[:, HGRN_DK:], zero_c)
        return lax.dot_general(z, zb, NT_DIMS, preferred_element_type=F32)

    def load_and_sum(j):
        rows = slice(j * CHUNK, (j + 1) * CHUNK)
        out = []
        for p in pairs:
            cols = slice(p * pair_w, (p + 1) * pair_w)
            qt, fg, v = q_ref[rows, cols].astype(F32), f_ref[rows, cols], v_ref[rows, cols]
            lf = jnp.log2(fg)
            pieces, rest = [], lf
            for _ in range(SPLIT):
                piece = rest.astype(BF16)
                pieces.append(piece)
                rest = rest - piece.astype(F32)
            sums = jnp.dot(dmat, jnp.concatenate(pieces, axis=0), preferred_element_type=F32)
            out.append((qt, fg, v, 1.0 - fg, sums))
        return out

    def intra_scores(staged):
        out = []
        for p in pairs:
            qt, fg, _, kk, sums = staged[p]
            scores = jnp.zeros((CHUNK, 2 * CHUNK), F32)
            for li, m in enumerate(LEVELS):
                decay = jnp.exp2(sums[(1 + li) * CHUNK:(2 + li) * CHUNK])
                if m >= 8:
                    blocks = [slice(i * m, (i + 1) * m) for i in range(CHUNK // m)]
                    zq = jnp.concatenate([qt[b] * decay[b] for b in blocks[1::2]], axis=0)
                    zk = jnp.concatenate([kk[b] * decay[b] if i % 2 == 0 else jnp.zeros((m, pair_w), F32)
                                          for i, b in enumerate(blocks)], axis=0).astype(BF16)
                    part = lax.dot_general(zq.astype(BF16),
                                           block_diag(zk[:, :HGRN_DK], zk[:, HGRN_DK:], zero_c),
                                           NT_DIMS, preferred_element_type=F32)
                    zero_rows = jnp.zeros((m, 2 * CHUNK), F32)
                    g = jnp.concatenate([part[(i // 2) * m:(i // 2 + 1) * m] if i % 2 else zero_rows
                                         for i in range(CHUNK // m)], axis=0)
                else:
                    z = jnp.where(((row // m) % 2) == 1, qt, kk) * decay
                    g = gram(z.astype(BF16))
                scores = jnp.where(lvl == int(np.log2(m)), g, scores)
            near = qt * fg * pltpu.roll(kk, 1, 0)
            diag = qt * kk
            per_head = lambda x: jnp.where(head0_cols,
                                           jnp.sum(x[:, :HGRN_DK], axis=1, keepdims=True),
                                           jnp.sum(x[:, HGRN_DK:], axis=1, keepdims=True))
            scores = jnp.where(lvl == 0, per_head(near), scores)
            out.append(jnp.where(lvl == -1, per_head(diag), scores).astype(BF16))
        return out

    def output_and_state(j, staged, scored):
        rows = slice(j * CHUNK, (j + 1) * CHUNK)
        outs = []
        for p in pairs:
            qt, _, v, kk, sums = staged[p]
            b = sums[0:CHUNK]
            qe = (qt * jnp.exp2(b)).astype(BF16)
            st_bd = block_diag(st_ref[2 * p].astype(BF16), st_ref[2 * p + 1].astype(BF16), zero_s)
            outs.append(jnp.dot(qe, st_bd, preferred_element_type=F32)
                        + jnp.dot(scored[p], block_diag(v[:, :HGRN_DK], v[:, HGRN_DK:], zero_c),
                                  preferred_element_type=F32))
            b_last = b[CHUNK - 1:CHUNK, :]
            kd = (kk * jnp.exp2(b_last - b)).astype(BF16)
            dec = jnp.exp2(b_last)
            for hh, sl in ((2 * p, slice(0, HGRN_DK)), (2 * p + 1, slice(HGRN_DK, pair_w))):
                upd = lax.dot_general(kd[:, sl], v[:, sl], TN_DIMS, preferred_element_type=F32)
                dec_rows = jnp.broadcast_to(dec[:, sl], (HGRN_DK, HGRN_DK)).T
                st_ref[hh] = st_ref[hh] * dec_rows + upd
        o_all = jnp.concatenate(outs, axis=1)
        a_ref[rows, :] = (_rmsnorm(o_all, gn_ref[...]) * gate_ref[rows, :].astype(F32)).astype(BF16)

    staged = {j: load_and_sum(j) for j in range(min(2, n_chunks))}
    scored = {0: intra_scores(staged[0])}
    for j in range(n_chunks):
        if j + 1 < n_chunks:
            scored[j + 1] = intra_scores(staged[j + 1])
        if j + 2 < n_chunks:
            staged[j + 2] = load_and_sum(j + 2)
        output_and_state(j, staged.pop(j), scored.pop(j))


def _hgrn_scan(q, f, v, gate, g_norm, bsz, seq):
    n_tok = q.shape[0]
    tc = SCAN_TILE
    steps = seq // tc
    toks = pl.BlockSpec((tc, D_MODEL), lambda b, c: (b * steps + c, 0))
    dmat = jnp.asarray(_decay_sum_matrix(), BF16)
    lvl = jnp.asarray(np.tile(_level_index(), (1, 2)))
    return pl.pallas_call(
        functools.partial(_hgrn_scan_kernel, n_chunks=tc // CHUNK),
        out_shape=jax.ShapeDtypeStruct((n_tok, D_MODEL), BF16),
        grid=(bsz, steps),
        in_specs=[toks, toks, toks, toks, _resident((1, D_MODEL)), _resident(dmat.shape),
                  _resident(lvl.shape)],
        out_specs=toks,
        scratch_shapes=[pltpu.VMEM((HGRN_HEADS, HGRN_DK, HGRN_DK), F32)],
        compiler_params=_params("parallel", "arbitrary"),
        name="hgrn_scan",
    )(q, f, v, gate, g_norm, dmat, lvl)


def kernel(x, positions, mix_norm, mlp_norm, final_norm, attn_w_qkv, attn_b_qkv, attn_sinks,
           attn_w_o, hgrn_w_in, hgrn_g_norm, hgrn_w_o, hgrn_lower_bounds, mlp_w_up, mlp_w_down):
    bsz, seq, _ = x.shape
    n_tok = bsz * seq
    xf = x.reshape(n_tok, D_MODEL)
    gain = lambda g: g.reshape(1, D_MODEL).astype(F32)

    qt, k, vt = _qkv_proj(xf, positions, gain(mix_norm[0]), attn_w_qkv[0].astype(BF16),
                          attn_b_qkv[0].reshape(1, QKV_DIM).astype(F32))
    a, w_o0, w_up0, w_down0, w_in, w_o1, w_up1, w_down1 = _attention(
        qt, k, vt, attn_sinks[0].astype(F32), bsz, seq,
        [(attn_w_o, 0), (mlp_w_up, 0), (mlp_w_down, 0), (hgrn_w_in, 0), (hgrn_w_o, 0),
         (mlp_w_up, 1), (mlp_w_down, 1)])
    xf, h1 = _out_mlp(a, xf, w_o0, gain(mlp_norm[0]), w_up0, w_down0, gain(mix_norm[1]), last=False)

    hq, hf, hv, gate = _hgrn_in(h1, w_in, hgrn_lower_bounds.astype(F32))
    a = _hgrn_scan(hq, hf, hv, gate, gain(hgrn_g_norm[0]), bsz, seq)
    (out,) = _out_mlp(a, xf, w_o1, gain(mlp_norm[1]), w_up1, w_down1, gain(final_norm), last=True)
    return out.reshape(bsz, seq, D_MODEL)
```
